```python
import math
import jax, jax.numpy as jnp
from jax import lax
import numpy as np

D_MODEL = 1024
BATCH = 32
SEQ = 256
DEPTH = 2
DEC_BATCH = 8
DEC_SEQ = 1024
PAST_LEN = 512

GRID_W = 64
NA_HEADS = 8
HEAD_DIM = 64
NA_WIDTH = NA_HEADS * HEAD_DIM
NA_MAX_ROWS = 8
NA_COLS = 16
NA_BAND = 2 * NA_COLS
S5_GROUPS = 16
S5_GROUP_CH = 16
S5_WIDTH = S5_GROUPS * S5_GROUP_CH
S5_STATE = 64
LRU_WIDTH = 256
LRU_BLOCKS = 4
LRU_BLOCK = LRU_WIDTH // LRU_BLOCKS
LRU_C = 8.0
LRU_CONV = 4
FFN_DIM = 2816
FFN_CONV = 3
ATTN_BLOCK = 128
N_BRANCH = 3
IN_SPLITS = (NA_WIDTH, 2 * NA_WIDTH, 3 * NA_WIDTH, 3 * NA_WIDTH + S5_WIDTH,
             3 * NA_WIDTH + S5_WIDTH + LRU_WIDTH, 3 * NA_WIDTH + S5_WIDTH + 2 * LRU_WIDTH)
IN_WIDTH = 3 * NA_WIDTH + S5_WIDTH + 2 * LRU_WIDTH + N_BRANCH * D_MODEL
EPS = 1e-6
NEG_INF = -1e30

kernel_name = "hybrid_flow_na_s5_rglru_step"


def rms_norm(x, g):
    x32 = x.astype(jnp.float32)
    y = x32 * lax.rsqrt(jnp.mean(x32 * x32, axis=-1, keepdims=True) + EPS)
    return y.astype(x.dtype) * g


def dw_conv(x, w, b, pad_left, pad_right):
    ch = x.shape[-1]
    y = lax.conv_general_dilated(x, w[:, None, :].astype(x.dtype), window_strides=(1,),
                                 padding=[(pad_left, pad_right)],
                                 dimension_numbers=('NWC', 'WIO', 'NWC'), feature_group_count=ch)
    return y + b


def linear_scan(a, b, h0, reverse):
    if reverse:
        a, b = jnp.flip(a, axis=1), jnp.flip(b, axis=1)

    def combine(left, right):
        a1, b1 = left
        a2, b2 = right
        return a1 * a2, a2 * b1 + b2

    a_cum, h_cum = lax.associative_scan(combine, (a, b), axis=1)
    h = h_cum + a_cum * h0[:, None]
    h_last = h[:, -1]
    if reverse:
        h = jnp.flip(h, axis=1)
    return h, h_last


def context_attention(q, k, v):
    bsz, length, heads, dh = q.shape
    nb = length // ATTN_BLOCK
    qb = (q * HEAD_DIM ** -0.5).reshape(bsz, nb, ATTN_BLOCK, heads, dh).swapaxes(0, 1)

    def block(qi):
        s = jnp.einsum('bqhd,bkhd->bhqk', qi, k).astype(jnp.float32)
        p = jax.nn.softmax(s, axis=-1).astype(v.dtype)
        return jnp.einsum('bhqk,bkhd->bqhd', p, v)

    out = lax.map(block, qb)
    return out.swapaxes(0, 1).reshape(bsz, length, heads * dh)


def neighbourhood_attention(q, k, v, k_ctx, v_ctx, rpb):
    bsz, length, heads, dh = q.shape
    rows = length // GRID_W
    wr = min(NA_MAX_ROWS, rows)
    ncb = GRID_W // NA_COLS
    r = jnp.arange(rows)
    row_idx = jnp.clip(r - wr // 2, 0, rows - wr)[:, None] + jnp.arange(wr)
    band0 = jnp.clip(jnp.arange(ncb) * NA_COLS - NA_COLS // 2, 0, GRID_W - NA_BAND)
    col_idx = band0[:, None] + jnp.arange(NA_BAND)
    qcol = jnp.arange(GRID_W).reshape(ncb, NA_COLS)
    win0 = jnp.clip(qcol - NA_COLS // 2, 0, GRID_W - NA_COLS)
    kc = col_idx[:, None, :]
    in_win = (kc >= win0[..., None]) & (kc < win0[..., None] + NA_COLS)
    dy = row_idx - r[:, None] + NA_MAX_ROWS - 1
    dx = jnp.clip(kc - qcol[..., None] + NA_COLS - 1, 0, 2 * NA_COLS - 2)
    bias = rpb[:, dy[:, None, None, :, None], dx[None, :, :, None, :]].astype(jnp.float32)

    k_grid = k.reshape(bsz, rows, GRID_W, heads, dh)
    v_grid = v.reshape(bsz, rows, GRID_W, heads, dh)
    gi_r = row_idx[:, :, None, None]
    gi_c = col_idx[None, None]
    kg = k_grid[:, gi_r, gi_c]
    vg = v_grid[:, gi_r, gi_c]
    qg = (q * HEAD_DIM ** -0.5).reshape(bsz, rows, ncb, NA_COLS, heads, dh)

    s_loc = jnp.einsum('brjqhd,brwjkhd->bhrjqwk', qg, kg).astype(jnp.float32) + bias
    s_loc = jnp.where(in_win[:, :, None, :], s_loc, NEG_INF)
    s_ctx = jnp.einsum('brjqhd,bchd->bhrjqc', qg, k_ctx).astype(jnp.float32)
    n_loc = wr * NA_BAND
    s = jnp.concatenate([s_loc.reshape(bsz, heads, rows, ncb, NA_COLS, n_loc), s_ctx], axis=-1)
    p = jax.nn.softmax(s, axis=-1).astype(v.dtype)
    p_loc = p[..., :n_loc].reshape(bsz, heads, rows, ncb, NA_COLS, wr, NA_BAND)
    out = (jnp.einsum('bhrjqwk,brwjkhd->brjqhd', p_loc, vg)
           + jnp.einsum('bhrjqc,bchd->brjqhd', p[..., n_loc:], v_ctx))
    return out.reshape(bsz, length, heads * dh)


def s5_mixer(u, lp, h0):
    bsz, length, _ = u.shape
    f32 = jnp.float32
    u32 = u.astype(f32).reshape(bsz, length, S5_GROUPS, S5_GROUP_CH)
    lam = lax.complex(lp['s5_lam_re'].astype(f32), lp['s5_lam_im'].astype(f32))
    step = jnp.exp(lp['s5_log_step'].astype(f32))[..., None]
    lam_bar = jnp.exp(lam * step)
    b_mat = lax.complex(lp['s5_b_re'].astype(f32), lp['s5_b_im'].astype(f32))
    b_bar = ((lam_bar - 1.0) / lam)[..., None] * b_mat[None]
    c_mat = lax.complex(lp['s5_c_re'].astype(f32), lp['s5_c_im'].astype(f32))
    bu = jnp.einsum('blgp,dgnp->dblgn', u32.astype(jnp.complex64), b_bar)
    y = lp['s5_d'].astype(f32).reshape(S5_GROUPS, S5_GROUP_CH) * u32
    finals = []
    for d, reverse in enumerate((False, True)):
        a = jnp.broadcast_to(lam_bar[d], bu[d].shape)
        h, h_last = linear_scan(a, bu[d], h0[:, d], reverse)
        y = y + jnp.einsum('blgn,gpn->blgp', h, c_mat[d]).real
        finals.append(h_last)
    y = jax.nn.gelu(y.reshape(bsz, length, S5_WIDTH))
    y = y * jax.nn.sigmoid(y @ lp['s5_w_glu'].astype(f32))
    return y.astype(u.dtype), jnp.stack(finals, axis=1)


def rglru_mixer(xr, gate, lp, h0):
    bsz, length, _ = xr.shape
    f32 = jnp.float32
    pad_l = LRU_CONV // 2
    xc = dw_conv(xr, lp['lru_conv_w'], lp['lru_conv_b'], pad_l, LRU_CONV - 1 - pad_l).astype(f32)
    xb = xc.reshape(bsz, length, LRU_BLOCKS, LRU_BLOCK)

    def block_diag_gate(w, b):
        y = jnp.einsum('blhi,dhij->dblhj', xb, w.astype(f32)).reshape(2, bsz, length, LRU_WIDTH)
        return jax.nn.sigmoid(y + b.astype(f32)[:, None, None, :])

    r = block_diag_gate(lp['lru_w_a'], lp['lru_b_a'])
    i = block_diag_gate(lp['lru_w_x'], lp['lru_b_x'])
    log_a = -LRU_C * r * jax.nn.softplus(-lp['lru_lam'].astype(f32))[:, None, None, :]
    a = jnp.exp(log_a)
    b = jnp.sqrt(-jnp.expm1(2.0 * log_a)) * (i * xc[None])
    h_f, last_f = linear_scan(a[0], b[0], h0[:, 0].astype(f32), False)
    h_b, last_b = linear_scan(a[1], b[1], h0[:, 1].astype(f32), True)
    y = (h_f + h_b) * jax.nn.gelu(gate.astype(f32))
    return y.astype(xr.dtype), jnp.stack([last_f, last_b], axis=1)


def token_mixer(h, lp, prefix):
    bsz, length, _ = h.shape
    z = h @ lp['w_in']
    q, k, v, u, xr, xg, g = jnp.split(z, IN_SPLITS, axis=-1)
    heads = (bsz, length, NA_HEADS, HEAD_DIM)
    q, k, v = q.reshape(heads), k.reshape(heads), v.reshape(heads)
    if prefix is None:
        attn = context_attention(q, k, v)
        s5_h0 = jnp.zeros((bsz, 2, S5_GROUPS, S5_STATE), jnp.complex64)
        lru_h0 = jnp.zeros((bsz, 2, LRU_WIDTH), jnp.float32)
    else:
        k_ctx, v_ctx, s5_h0, lru_h0 = prefix
        attn = neighbourhood_attention(q, k, v, k_ctx, v_ctx, lp['rpb'])
    s5_y, s5_last = s5_mixer(u, lp, s5_h0)
    lru_y, lru_last = rglru_mixer(xr, xg, lp, lru_h0)
    gates = jax.nn.sigmoid(g.reshape(bsz, length, N_BRANCH, D_MODEL).astype(jnp.float32)).astype(h.dtype)
    merged = (gates[:, :, 0] * (attn @ lp['w_br_attn'])
              + gates[:, :, 1] * (s5_y @ lp['w_br_s5'])
              + gates[:, :, 2] * (lru_y @ lp['w_br_lru']))
    return merged @ lp['w_out'], (k, v, s5_last, lru_last)


def conv_ffn(h, lp):
    a, b = jnp.split(h @ lp['ffn_w_up'], 2, axis=-1)
    a = dw_conv(a, lp['ffn_conv_w'], lp['ffn_conv_b'], FFN_CONV // 2, FFN_CONV - 1 - FFN_CONV // 2)
    return (jax.nn.gelu(a) * b) @ lp['ffn_w_down']


def layer(x, cond, lp, prefix):
    mod = jax.nn.silu(cond) @ lp['w_ada'] + lp['b_ada']
    sh1, sc1, gt1, sh2, sc2, gt2 = jnp.split(mod[:, None, :], 6, axis=-1)
    h = rms_norm(x, lp['g1']) * (1 + sc1) + sh1
    m, ctx_state = token_mixer(h, lp, prefix)
    x = x + gt1 * m
    h = rms_norm(x, lp['g2']) * (1 + sc2) + sh2
    x = x + gt2 * conv_ffn(h, lp)
    return x, ctx_state


def setup_inputs(seed: int = 0) -> dict:
    key = jax.random.key(seed)
    keys = iter(jax.random.split(key, 64))
    f32 = jnp.float32

    def nrm(shape, scale=1.0):
        return scale * jax.random.normal(next(keys), shape, f32)

    def unif(shape, lo, hi):
        return jax.random.uniform(next(keys), shape, f32, lo, hi)

    L = DEPTH
    n_idx = jnp.arange(S5_STATE, dtype=f32)
    lru_s = unif((L, 2, LRU_WIDTH), 0.9, 0.999) ** (1.0 / LRU_C)
    return {
        'x_prompt': nrm((BATCH, SEQ, D_MODEL)),
        'x_sample': nrm((DEC_BATCH, DEC_SEQ, D_MODEL)),
        'cache_k': nrm((DEC_BATCH, L, PAST_LEN, NA_HEADS, HEAD_DIM)),
        'cache_v': nrm((DEC_BATCH, L, PAST_LEN, NA_HEADS, HEAD_DIM)),
        'state_s5_re': nrm((DEC_BATCH, L, 2, S5_GROUPS, S5_STATE), 0.1),
        'state_s5_im': nrm((DEC_BATCH, L, 2, S5_GROUPS, S5_STATE), 0.1),
        'state_lru': nrm((DEC_BATCH, L, 2, LRU_WIDTH), 0.5),
        'c': nrm((DEC_BATCH, D_MODEL)),
        'c_ctx': nrm((D_MODEL,)),
        'w_ada': nrm((L, D_MODEL, 6 * D_MODEL), 0.5 * D_MODEL ** -0.5),
        'b_ada': nrm((L, 6 * D_MODEL), 0.01),
        'g_norm1': 1.0 + nrm((L, D_MODEL), 0.01),
        'g_norm2': 1.0 + nrm((L, D_MODEL), 0.01),
        'w_in': nrm((L, D_MODEL, IN_WIDTH), D_MODEL ** -0.5),
        'rpb': nrm((L, NA_HEADS, 2 * NA_MAX_ROWS - 1, 2 * NA_COLS - 1), 0.1),
        's5_lam_re': -0.5 + nrm((L, 2, S5_GROUPS, S5_STATE), 0.01),
        's5_lam_im': math.pi * n_idx + nrm((L, 2, S5_GROUPS, S5_STATE), 0.01),
        's5_log_step': jnp.log(unif((L, 2, S5_GROUPS), 0.001, 0.1)),
        's5_b_re': nrm((L, S5_GROUPS, S5_STATE, S5_GROUP_CH), (2 * S5_GROUP_CH) ** -0.5),
        's5_b_im': nrm((L, S5_GROUPS, S5_STATE, S5_GROUP_CH), (2 * S5_GROUP_CH) ** -0.5),
        's5_c_re': nrm((L, 2, S5_GROUPS, S5_GROUP_CH, S5_STATE), S5_STATE ** -0.5),
        's5_c_im': nrm((L, 2, S5_GROUPS, S5_GROUP_CH, S5_STATE), S5_STATE ** -0.5),
        's5_d': nrm((L, S5_WIDTH)),
        's5_w_glu': nrm((L, S5_WIDTH, S5_WIDTH), S5_WIDTH ** -0.5),
        'lru_conv_w': nrm((L, LRU_CONV, LRU_WIDTH), LRU_CONV ** -0.5),
        'lru_conv_b': nrm((L, LRU_WIDTH), 0.01),
        'lru_w_a': nrm((L, 2, LRU_BLOCKS, LRU_BLOCK, LRU_BLOCK), LRU_BLOCK ** -0.5),
        'lru_b_a': nrm((L, 2, LRU_WIDTH), 0.01),
        'lru_w_x': nrm((L, 2, LRU_BLOCKS, LRU_BLOCK, LRU_BLOCK), LRU_BLOCK ** -0.5),
        'lru_b_x': nrm((L, 2, LRU_WIDTH), 0.01),
        'lru_lam': jnp.log(lru_s) - jnp.log1p(-lru_s),
        'w_br_attn': nrm((L, NA_WIDTH, D_MODEL), NA_WIDTH ** -0.5),
        'w_br_s5': nrm((L, S5_WIDTH, D_MODEL), S5_WIDTH ** -0.5),
        'w_br_lru': nrm((L, LRU_WIDTH, D_MODEL), LRU_WIDTH ** -0.5),
        'w_out': nrm((L, D_MODEL, D_MODEL), D_MODEL ** -0.5),
        'ffn_w_up': nrm((L, D_MODEL, 2 * FFN_DIM), D_MODEL ** -0.5),
        'ffn_conv_w': nrm((L, FFN_CONV, FFN_DIM), FFN_CONV ** -0.5),
        'ffn_conv_b': nrm((L, FFN_DIM), 0.01),
        'ffn_w_down': nrm((L, FFN_DIM, D_MODEL), FFN_DIM ** -0.5),
        'g_final': 1.0 + nrm((D_MODEL,), 0.01),
    }


def reference(x_prompt, x_sample, cache_k, cache_v, state_s5_re, state_s5_im, state_lru, c, c_ctx,
              w_ada, b_ada, g_norm1, g_norm2, w_in, rpb,
              s5_lam_re, s5_lam_im, s5_log_step, s5_b_re, s5_b_im, s5_c_re, s5_c_im, s5_d, s5_w_glu,
              lru_conv_w, lru_conv_b, lru_w_a, lru_b_a, lru_w_x, lru_b_x, lru_lam,
              w_br_attn, w_br_s5, w_br_lru, w_out,
              ffn_w_up, ffn_conv_w, ffn_conv_b, ffn_w_down, g_final):
    f32 = jnp.float32
    yp, ys = x_prompt, x_sample
    ks, vs, s5r, s5i, lrus = [], [], [], [], []
    for l in range(DEPTH):
        lp = {
            'w_ada': w_ada[l], 'b_ada': b_ada[l], 'g1': g_norm1[l], 'g2': g_norm2[l],
            'w_in': w_in[l], 'rpb': rpb[l],
            's5_lam_re': s5_lam_re[l], 's5_lam_im': s5_lam_im[l], 's5_log_step': s5_log_step[l],
            's5_b_re': s5_b_re[l], 's5_b_im': s5_b_im[l], 's5_c_re': s5_c_re[l], 's5_c_im': s5_c_im[l],
            's5_d': s5_d[l], 's5_w_glu': s5_w_glu[l],
            'lru_conv_w': lru_conv_w[l], 'lru_conv_b': lru_conv_b[l],
            'lru_w_a': lru_w_a[l], 'lru_b_a': lru_b_a[l], 'lru_w_x': lru_w_x[l], 'lru_b_x': lru_b_x[l],
            'lru_lam': lru_lam[l],
            'w_br_attn': w_br_attn[l], 'w_br_s5': w_br_s5[l], 'w_br_lru': w_br_lru[l], 'w_out': w_out[l],
            'ffn_w_up': ffn_w_up[l], 'ffn_conv_w': ffn_conv_w[l], 'ffn_conv_b': ffn_conv_b[l],
            'ffn_w_down': ffn_w_down[l],
        }
        yp, (k_l, v_l, s5_last, lru_last) = layer(yp, c_ctx[None], lp, None)
        ks.append(k_l)
        vs.append(v_l)
        s5r.append(s5_last.real)
        s5i.append(s5_last.imag)
        lrus.append(lru_last)
        prefix = (cache_k[:, l], cache_v[:, l],
                  lax.complex(state_s5_re[:, l].astype(f32), state_s5_im[:, l].astype(f32)),
                  state_lru[:, l])
        ys, _ = layer(ys, c, lp, prefix)
    y_prompt = rms_norm(yp, g_final)
    y_sample = rms_norm(ys, g_final)
    return (y_prompt, y_sample, jnp.stack(ks, axis=1), jnp.stack(vs, axis=1),
            jnp.stack(s5r, axis=1), jnp.stack(s5i, axis=1), jnp.stack(lrus, axis=1))
```

```python
import functools

import jax
import jax.numpy as jnp
from jax import lax
from jax.experimental import pallas as pl
from jax.experimental.pallas import tpu as pltpu

D_MODEL = 1024
DEPTH = 2
GRID_W = 64
NA_HEADS = 8
HEAD_DIM = 64
NA_WIDTH = NA_HEADS * HEAD_DIM
NA_MAX_ROWS = 8
NA_COLS = 16
S5_GROUPS = 16
S5_GROUP_CH = 16
S5_WIDTH = S5_GROUPS * S5_GROUP_CH
S5_STATE = 64
S5_LANES = S5_GROUPS * S5_STATE
LRU_WIDTH = 256
LRU_BLOCKS = 4
LRU_BLOCK = LRU_WIDTH // LRU_BLOCKS
LRU_C = 8.0
LRU_CONV = 4
FFN_DIM = 2816
FFN_CONV = 3
N_BRANCH = 3
MIX_WIDTH = 3 * NA_WIDTH + S5_WIDTH + 2 * LRU_WIDTH
EPS = 1e-6
NEG_INF = -1e30

SUBLANES = 8
COND_ROWS = 16
VMEM_LIMIT = 56 * 1024 * 1024

F32 = jnp.float32
BF16 = jnp.bfloat16


def _params(*semantics):
    return pltpu.CompilerParams(dimension_semantics=semantics, vmem_limit_bytes=VMEM_LIMIT)


def _dot(a, b):
    return jnp.dot(a, b, preferred_element_type=F32)


def _dot_nt(a, b):
    return lax.dot_general(a, b, (((1,), (1,)), ((), ())), preferred_element_type=F32)


def _rms(x):
    return x * lax.rsqrt(jnp.mean(x * x, axis=-1, keepdims=True) + EPS)


def _gelu(x):
    return jax.nn.gelu(x)


def _ada_kernel(cond_ref, w_ref, b_ref, o_ref):
    c = cond_ref[...]
    s = c * jax.nn.sigmoid(c)
    o_ref[0] = _dot(s.astype(BF16), w_ref[0].astype(BF16)) + b_ref[0]


def _ada(cond, w_ada, b_ada):
    nb = 1536
    n = 6 * D_MODEL
    return pl.pallas_call(
        _ada_kernel,
        grid=(DEPTH, n // nb),
        in_specs=[
            pl.BlockSpec((COND_ROWS, D_MODEL), lambda l, j: (0, 0)),
            pl.BlockSpec((1, D_MODEL, nb), lambda l, j: (l, 0, j)),
            pl.BlockSpec((1, 1, nb), lambda l, j: (l, 0, j)),
        ],
        out_specs=pl.BlockSpec((1, COND_ROWS, nb), lambda l, j: (l, 0, j)),
        out_shape=jax.ShapeDtypeStruct((DEPTH, COND_ROWS, n), F32),
        compiler_params=_params("arbitrary", "arbitrary"),
        name="ada",
    )(cond, w_ada, b_ada.reshape(DEPTH, 1, n))


def _in_kernel(x_ref, mod_ref, g_ref, w_ref, cw_ref, cb_ref,
               q_ref, k_ref, v_ref, u_ref, xc_ref, xg_ref, pad_ref, *, seq_len):
    tm = x_ref.shape[0]
    md = mod_ref[0]
    h = (_rms(x_ref[...]) * g_ref[...]) * (1.0 + md[1:2]) + md[0:1]
    z = _dot(h.astype(BF16), w_ref[...])
    q_ref[...] = z[:, 0:NA_WIDTH].astype(q_ref.dtype)
    k_ref[...] = z[:, NA_WIDTH:2 * NA_WIDTH].astype(k_ref.dtype)
    v_ref[...] = z[:, 2 * NA_WIDTH:3 * NA_WIDTH].astype(v_ref.dtype)
    o = 3 * NA_WIDTH
    u_ref[...] = z[:, o:o + S5_WIDTH]
    xg_ref[...] = z[:, o + S5_WIDTH + LRU_WIDTH:o + S5_WIDTH + 2 * LRU_WIDTH]
    xr = z[:, o + S5_WIDTH:o + S5_WIDTH + LRU_WIDTH]
    pad_ref[0:SUBLANES, :] = jnp.zeros((SUBLANES, LRU_WIDTH), F32)
    pad_ref[SUBLANES + tm:2 * SUBLANES + tm, :] = jnp.zeros((SUBLANES, LRU_WIDTH), F32)
    pad_ref[SUBLANES:SUBLANES + tm, :] = xr
    t = lax.broadcasted_iota(jnp.int32, (tm, 1), 0) & (seq_len - 1)
    acc = jnp.zeros((tm, LRU_WIDTH), F32) + cb_ref[...]
    pad_l = LRU_CONV // 2
    for kk in range(LRU_CONV):
        s = kk - pad_l
        xs = xr if s == 0 else pad_ref[SUBLANES + s:SUBLANES + s + tm, :]
        valid = jnp.logical_and(t + s >= 0, t + s < seq_len)
        acc = acc + cw_ref[kk:kk + 1, :] * jnp.where(valid, xs, 0.0)
    xc_ref[...] = acc


def _in_proj(x, mod6, mod_row, g, w_mix, conv_w, conv_b, seq_len, kv_dtype):
    t_tokens = x.shape[0]
    tm = 1024
    tok = lambda i: (i, 0)
    full = lambda i: (0, 0)
    widths = (NA_WIDTH, NA_WIDTH, NA_WIDTH, S5_WIDTH, LRU_WIDTH, LRU_WIDTH)
    dtypes = (BF16, kv_dtype, kv_dtype, F32, F32, F32)
    return pl.pallas_call(
        functools.partial(_in_kernel, seq_len=seq_len),
        grid=(t_tokens // tm,),
        in_specs=[
            pl.BlockSpec((tm, D_MODEL), tok),
            pl.BlockSpec((1, 6, D_MODEL), lambda i: (mod_row(i, tm), 0, 0)),
            pl.BlockSpec((1, D_MODEL), full),
            pl.BlockSpec((D_MODEL, MIX_WIDTH), full),
            pl.BlockSpec((LRU_CONV, LRU_WIDTH), full),
            pl.BlockSpec((1, LRU_WIDTH), full),
        ],
        out_specs=[pl.BlockSpec((tm, w), tok) for w in widths],
        out_shape=[jax.ShapeDtypeStruct((t_tokens, w), d) for w, d in zip(widths, dtypes)],
        scratch_shapes=[pltpu.VMEM((tm + 2 * SUBLANES, LRU_WIDTH), F32)],
        compiler_params=_params("arbitrary"),
        name="in_proj",
    )(x, mod6, g, w_mix, conv_w, conv_b)


def _softmax_rows(parts):
    m = parts[0].max(axis=-1, keepdims=True)
    for p in parts[1:]:
        m = jnp.maximum(m, p.max(axis=-1, keepdims=True))
    es = [jnp.exp(p - m) for p in parts]
    den = es[0].sum(axis=-1, keepdims=True)
    for e in es[1:]:
        den = den + e.sum(axis=-1, keepdims=True)
    inv = 1.0 / den
    return [(e * inv).astype(BF16) for e in es]


def _head_pair_queries(q2):
    first = lax.broadcasted_iota(jnp.int32, (1, 2 * HEAD_DIM), 1) < HEAD_DIM
    zero = jnp.zeros_like(q2)
    return jnp.concatenate([jnp.where(first, q2, zero), jnp.where(first, zero, q2)], axis=0), first


def _ctx_attn_kernel(q_ref, k_ref, v_ref, o_ref):
    length = q_ref.shape[1]
    scale = HEAD_DIM ** -0.5
    for j in range(NA_HEADS // 2):
        cols = slice(2 * HEAD_DIM * j, 2 * HEAD_DIM * (j + 1))
        qs, first = _head_pair_queries(q_ref[0, :, cols])
        k2 = k_ref[0, :, cols].astype(BF16)
        v2 = v_ref[0, :, cols].astype(BF16)
        s = _dot_nt(qs, k2) * scale
        (p,) = _softmax_rows([s])
        o = _dot(p, v2)
        o_ref[0, :, cols] = jnp.where(first, o[0:length], o[length:2 * length]).astype(o_ref.dtype)


def _ctx_attention(q, k, v):
    bsz, length, _ = q.shape
    spec = pl.BlockSpec((1, length, NA_WIDTH), lambda b: (b, 0, 0))
    return pl.pallas_call(
        _ctx_attn_kernel,
        grid=(bsz,),
        in_specs=[spec, spec, spec],
        out_specs=spec,
        out_shape=jax.ShapeDtypeStruct((bsz, length, NA_WIDTH), BF16),
        compiler_params=_params("arbitrary"),
        name="ctx_attn",
    )(q, k, v)


def _na_rows(length):
    rows = length // GRID_W
    return rows, min(NA_MAX_ROWS, rows)


def _na_attn_kernel(q_ref, k_ref, v_ref, kc_ref, vc_ref, tbl_ref, o_ref):
    rows, wr = _na_rows(q_ref.shape[1])
    scale = HEAD_DIM ** -0.5
    kc = kc_ref[0, 0].astype(BF16)
    vc = vc_ref[0, 0].astype(BF16)
    for r in range(rows):
        start = min(max(r - wr // 2, 0), rows - wr) * GRID_W
        qs, first = _head_pair_queries(q_ref[0, r * GRID_W:(r + 1) * GRID_W, :])
        kl = k_ref[0, start:start + wr * GRID_W, :]
        vl = v_ref[0, start:start + wr * GRID_W, :]
        s_loc = _dot_nt(qs, kl) * scale + tbl_ref[0, r]
        s_ctx = _dot_nt(qs, kc) * scale
        p_loc, p_ctx = _softmax_rows([s_loc, s_ctx])
        o = _dot(p_loc, vl) + _dot(p_ctx, vc)
        o_ref[0, r * GRID_W:(r + 1) * GRID_W, :] = jnp.where(
            first, o[0:GRID_W], o[GRID_W:2 * GRID_W]).astype(o_ref.dtype)


def _na_bias_table(rpb, length):
    rows, wr = _na_rows(length)
    r = jnp.arange(rows)
    krow = jnp.clip(r - wr // 2, 0, rows - wr)[:, None] + jnp.arange(wr)
    dy = krow - r[:, None] + NA_MAX_ROWS - 1
    qc = jnp.arange(GRID_W)[:, None]
    kc = jnp.arange(GRID_W)[None, :]
    win0 = jnp.clip(qc - NA_COLS // 2, 0, GRID_W - NA_COLS)
    in_win = (kc >= win0) & (kc < win0 + NA_COLS)
    dx = jnp.clip(kc - qc + NA_COLS - 1, 0, 2 * NA_COLS - 2)
    bias = rpb[:, dy[:, None, :, None], dx[None, :, None, :]].astype(F32)
    tbl = jnp.where(in_win[None, None, :, None, :], bias, NEG_INF)
    tbl = tbl.reshape(NA_HEADS // 2, 2, rows, GRID_W, wr * GRID_W)
    return tbl.transpose(0, 2, 1, 3, 4).reshape(NA_HEADS // 2, rows, 2 * GRID_W, wr * GRID_W)


def _na_attention(q, k, v, cache_k, cache_v, layer, tbl):
    bsz, length, _ = q.shape
    ctx = cache_k.shape[2]
    rows, wr = _na_rows(length)
    pair = 2 * HEAD_DIM
    tok = pl.BlockSpec((1, length, pair), lambda j, b: (b, 0, j))
    cache = pl.BlockSpec((1, 1, ctx, pair), lambda j, b: (b, layer, 0, j))
    return pl.pallas_call(
        _na_attn_kernel,
        grid=(NA_HEADS // 2, bsz),
        in_specs=[tok, tok, tok, cache, cache,
                  pl.BlockSpec((1, rows, 2 * GRID_W, wr * GRID_W), lambda j, b: (j, 0, 0, 0))],
        out_specs=tok,
        out_shape=jax.ShapeDtypeStruct((bsz, length, NA_WIDTH), BF16),
        compiler_params=_params("arbitrary", "arbitrary"),
        name="na_attn",
    )(q, k, v, cache_k, cache_v, tbl)


LANES = 128


def _to_time_major(src_ref, dst_ref):
    tc = src_ref.shape[1]
    for b in range(SUBLANES):
        for j in range(dst_ref.shape[0]):
            dst_ref[j, pl.ds(b, tc, stride=SUBLANES), :] = src_ref[b, :, j * LANES:(j + 1) * LANES]
    return jnp.concatenate([dst_ref[j] for j in range(dst_ref.shape[0])], axis=-1)


def _from_time_major(src_ref, store):
    tc = src_ref.shape[1] // SUBLANES
    for b in range(SUBLANES):
        for j in range(src_ref.shape[0]):
            store(b, j, src_ref[j, pl.ds(b, tc, stride=SUBLANES), :])


def _s5_prep_kernel(lre_ref, lim_ref, step_ref, bre_ref, bim_ref, lam_ref, bb_ref):
    lre = lre_ref[...]
    lim = lim_ref[...]
    step = jnp.exp(step_ref[...])
    mag = jnp.exp(lre * step)
    ang = lim * step
    ar = mag * jnp.cos(ang)
    ai = mag * jnp.sin(ang)
    lam_ref[0] = ar
    lam_ref[1] = ai
    den = lre * lre + lim * lim
    kr = ((ar - 1.0) * lre + ai * lim) / den
    ki = (ai * lre - (ar - 1.0) * lim) / den
    for d in range(2):
        bb_ref[d, 0] = kr[d:d + 1] * bre_ref[...] - ki[d:d + 1] * bim_ref[...]
        bb_ref[d, 1] = kr[d:d + 1] * bim_ref[...] + ki[d:d + 1] * bre_ref[...]


def _s5_prep(lam_re, lam_im, log_step, b_re, b_im):
    step = jnp.broadcast_to(log_step[..., None], lam_re.shape).reshape(2, S5_LANES)
    bt = lambda b: b.transpose(2, 0, 1).reshape(S5_GROUP_CH, S5_LANES)
    return pl.pallas_call(
        _s5_prep_kernel,
        out_shape=[jax.ShapeDtypeStruct((2, 2, S5_LANES), F32),
                   jax.ShapeDtypeStruct((2, 2, S5_GROUP_CH, S5_LANES), F32)],
        name="s5_prep",
    )(lam_re.reshape(2, S5_LANES), lam_im.reshape(2, S5_LANES), step, bt(b_re), bt(b_im))


def _s5_matrices(b_bar, c_re, c_im):
    eye = jnp.eye(S5_GROUPS, dtype=F32)
    bb = b_bar.reshape(2, 2, S5_GROUP_CH, S5_GROUPS, S5_STATE)
    bd = jnp.einsum('dcpgn,gh->dgpchn', bb, eye)
    bd = bd.reshape(2, S5_WIDTH, 2 * S5_LANES).astype(BF16)
    cc = jnp.stack([c_re, c_im], axis=1)
    cd = jnp.einsum('dcgpn,gh->dcgnhp', cc, eye)
    cd = cd.reshape(2, 2, S5_LANES, S5_WIDTH).astype(BF16)
    return bd, cd


def _s5_kernel(u_ref, bd_ref, cd_ref, lam_ref, dvec_ref, h0_ref, y_ref, hl_ref,
               ui_ref, bu_ref, yi_ref, st_ref, *, tc):
    d = pl.program_id(1)
    c = pl.program_id(2)

    @pl.when(c == 0)
    def _():
        st_ref[...] = h0_ref[0]

    u = _to_time_major(u_ref, ui_ref)
    bu_ref[...] = _dot(u.astype(BF16), bd_ref[0])
    shape = (SUBLANES, S5_LANES)
    lr = jnp.broadcast_to(lam_ref[0, 0:1, :], shape)
    li = jnp.broadcast_to(lam_ref[0, 1:2, :], shape)

    def step(t, carry):
        hr, hi = carry
        tt = jnp.where(d == 0, t, tc - 1 - t)
        rows = pl.ds(pl.multiple_of(tt * SUBLANES, SUBLANES), SUBLANES)
        nr = (lr * hr - li * hi) + bu_ref[rows, 0:S5_LANES]
        ni = (lr * hi + li * hr) + bu_ref[rows, S5_LANES:2 * S5_LANES]
        bu_ref[rows, 0:S5_LANES] = nr
        bu_ref[rows, S5_LANES:2 * S5_LANES] = ni
        return nr, ni

    hr, hi = lax.fori_loop(0, tc, step, (st_ref[0], st_ref[1]), unroll=4)
    st_ref[0] = hr
    st_ref[1] = hi
    hl_ref[0, 0] = hr
    hl_ref[0, 1] = hi
    hb = bu_ref[...].astype(BF16)
    y = _dot(hb[:, 0:S5_LANES], cd_ref[0, 0]) - _dot(hb[:, S5_LANES:2 * S5_LANES], cd_ref[0, 1])
    skip = jnp.where(d == 0, 1.0, 0.0)
    y = y + skip * (dvec_ref[...] * u)
    for j in range(yi_ref.shape[0]):
        yi_ref[j] = y[:, j * LANES:(j + 1) * LANES]

    def store(b, j, val):
        y_ref[0, b, :, j * LANES:(j + 1) * LANES] = val

    _from_time_major(yi_ref, store)


def _s5_scan(u, bd, cd, lam_bar, dvec, h0):
    bsz, length, _ = u.shape
    tc = 64
    nc = length // tc
    chunk = lambda d, c: c + d * (nc - 1 - 2 * c)
    return pl.pallas_call(
        functools.partial(_s5_kernel, tc=tc),
        grid=(bsz // SUBLANES, 2, nc),
        in_specs=[
            pl.BlockSpec((SUBLANES, tc, S5_WIDTH), lambda g, d, c: (g, chunk(d, c), 0)),
            pl.BlockSpec((1, S5_WIDTH, 2 * S5_LANES), lambda g, d, c: (d, 0, 0)),
            pl.BlockSpec((1, 2, S5_LANES, S5_WIDTH), lambda g, d, c: (d, 0, 0, 0)),
            pl.BlockSpec((1, 2, S5_LANES), lambda g, d, c: (d, 0, 0)),
            pl.BlockSpec((1, S5_WIDTH), lambda g, d, c: (0, 0)),
            pl.BlockSpec((1, 2, SUBLANES, S5_LANES), lambda g, d, c: (d, 0, g, 0)),
        ],
        out_specs=[
            pl.BlockSpec((1, SUBLANES, tc, S5_WIDTH), lambda g, d, c: (d, g, chunk(d, c), 0)),
            pl.BlockSpec((1, 2, SUBLANES, S5_LANES), lambda g, d, c: (d, 0, g, 0)),
        ],
        out_shape=[jax.ShapeDtypeStruct((2, bsz, length, S5_WIDTH), F32),
                   jax.ShapeDtypeStruct((2, 2, bsz, S5_LANES), F32)],
        scratch_shapes=[
            pltpu.VMEM((S5_WIDTH // LANES, tc * SUBLANES, LANES), F32),
            pltpu.VMEM((tc * SUBLANES, 2 * S5_LANES), F32),
            pltpu.VMEM((S5_WIDTH // LANES, tc * SUBLANES, LANES), F32),
            pltpu.VMEM((2, SUBLANES, S5_LANES), F32),
        ],
        compiler_params=_params("arbitrary", "arbitrary", "arbitrary"),
        name="s5_scan",
    )(u, bd, cd, lam_bar, dvec, h0)


def _lru_kernel(xc_ref, wg_ref, bg_ref, lam_ref, h0_ref, h_ref, hl_ref,
                xi_ref, a_ref, b_ref, hi_ref, st_ref, *, tc):
    d = pl.program_id(1)
    c = pl.program_id(2)

    @pl.when(c == 0)
    def _():
        st_ref[...] = h0_ref[0]

    x = _to_time_major(xc_ref, xi_ref)
    gates =_dot(x.astype(BF16), wg_ref[0]) + bg_ref[0]
    r = jax.nn.sigmoid(gates[:, 0:LRU_WIDTH])
    i = jax.nn.sigmoid(gates[:, LRU_WIDTH:2 * LRU_WIDTH])
    z = -lam_ref[0]
    softplus = jnp.maximum(z, 0.0) + jnp.log1p(jnp.exp(-jnp.abs(z)))
    log_a = -LRU_C * r * softplus
    th = jnp.tanh(log_a)
    neg_expm1 = -2.0 * th / (1.0 - th)
    a_ref[...] = jnp.exp(log_a)
    b_ref[...] = jnp.sqrt(neg_expm1) * (i * x)

    def step(t, h):
        tt = jnp.where(d == 0, t, tc - 1 - t)
        rows = pl.ds(pl.multiple_of(tt * SUBLANES, SUBLANES), SUBLANES)
        h = a_ref[rows, :] * h + b_ref[rows, :]
        for j in range(hi_ref.shape[0]):
            hi_ref[j, rows, :] = h[:, j * LANES:(j + 1) * LANES]
        return h

    h = lax.fori_loop(0, tc, step, st_ref[...], unroll=8)
    st_ref[...] = h
    hl_ref[0] = h

    def store(b, j, val):
        h_ref[0, b, :, j * LANES:(j + 1) * LANES] = val

    _from_time_major(hi_ref, store)


def _lru_gate_matrices(w_a, b_a, w_x, b_x):
    eye = jnp.eye(LRU_BLOCKS, dtype=F32)
    dense = lambda w: jnp.einsum('dhij,hk->dhikj', w, eye).reshape(2, LRU_WIDTH, LRU_WIDTH)
    wg = jnp.concatenate([dense(w_a), dense(w_x)], axis=-1).astype(BF16)
    bg = jnp.concatenate([b_a, b_x], axis=-1).reshape(2, 1, 2 * LRU_WIDTH)
    return wg, bg


def _lru_scan(xc, wg, bg, lam, h0):
    bsz, length, _ = xc.shape
    tc = 128
    nc = length // tc
    chunk = lambda d, c: c + d * (nc - 1 - 2 * c)
    rows = tc * SUBLANES
    return pl.pallas_call(
        functools.partial(_lru_kernel, tc=tc),
        grid=(bsz // SUBLANES, 2, nc),
        in_specs=[
            pl.BlockSpec((SUBLANES, tc, LRU_WIDTH), lambda g, d, c: (g, chunk(d, c), 0)),
            pl.BlockSpec((1, LRU_WIDTH, 2 * LRU_WIDTH), lambda g, d, c: (d, 0, 0)),
            pl.BlockSpec((1, 1, 2 * LRU_WIDTH), lambda g, d, c: (d, 0, 0)),
            pl.BlockSpec((1, 1, LRU_WIDTH), lambda g, d, c: (d, 0, 0)),
            pl.BlockSpec((1, SUBLANES, LRU_WIDTH), lambda g, d, c: (d, g, 0)),
        ],
        out_specs=[
            pl.BlockSpec((1, SUBLANES, tc, LRU_WIDTH), lambda g, d, c: (d, g, chunk(d, c), 0)),
            pl.BlockSpec((1, SUBLANES, LRU_WIDTH), lambda g, d, c: (d, g, 0)),
        ],
        out_shape=[jax.ShapeDtypeStruct((2, bsz, length, LRU_WIDTH), F32),
                   jax.ShapeDtypeStruct((2, bsz, LRU_WIDTH), F32)],
        scratch_shapes=[
            pltpu.VMEM((LRU_WIDTH // LANES, rows, LANES), F32),
            pltpu.VMEM((rows, LRU_WIDTH), F32),
            pltpu.VMEM((rows, LRU_WIDTH), F32),
            pltpu.VMEM((LRU_WIDTH // LANES, rows, LANES), F32),
            pltpu.VMEM((SUBLANES, LRU_WIDTH), F32),
        ],
        compiler_params=_params("arbitrary", "arbitrary", "arbitrary"),
        name="lru_scan",
    )(xc, wg, bg, lam.reshape(2, 1, LRU_WIDTH), h0)


def _merge_kernel(x_ref, mod_ref, g1_ref, g2_ref, attn_ref, s5_ref, lru_ref, xg_ref,
                  wgate_ref, wglu_ref, wba_ref, wbs_ref, wbl_ref, wout_ref, x1_ref, h2_ref):
    md = mod_ref[0]
    x = x_ref[...]
    h = (_rms(x) * g1_ref[...]) * (1.0 + md[1:2]) + md[0:1]
    gates = jax.nn.sigmoid(_dot(h.astype(BF16), wgate_ref[...]))
    ys = _gelu(s5_ref[0] + s5_ref[1])
    s5_y = ys * jax.nn.sigmoid(_dot(ys.astype(BF16), wglu_ref[...]))
    lru_y = (lru_ref[0] + lru_ref[1]) * _gelu(xg_ref[...])
    merged = (gates[:, 0:D_MODEL] * _dot(attn_ref[...], wba_ref[...])
              + gates[:, D_MODEL:2 * D_MODEL] * _dot(s5_y.astype(BF16), wbs_ref[...])
              + gates[:, 2 * D_MODEL:3 * D_MODEL] * _dot(lru_y.astype(BF16), wbl_ref[...]))
    x1 = x + md[2:3] * _dot(merged.astype(BF16), wout_ref[...])
    x1_ref[...] = x1
    h2_ref[...] = ((_rms(x1) * g2_ref[...]) * (1.0 + md[4:5]) + md[3:4]).astype(h2_ref.dtype)


def _merge(x, mod6, mod_row, g1, g2, attn, s5_y, lru_h, xg, w_gate, w_glu, w_ba, w_bs, w_bl, w_out):
    t_tokens = x.shape[0]
    tm = 512
    tok = lambda i: (i, 0)
    full = lambda i: (0, 0)
    wspec = lambda w: pl.BlockSpec(w.shape, full)
    return pl.pallas_call(
        _merge_kernel,
        grid=(t_tokens // tm,),
        in_specs=[
            pl.BlockSpec((tm, D_MODEL), tok),
            pl.BlockSpec((1, 6, D_MODEL), lambda i: (mod_row(i, tm), 0, 0)),
            pl.BlockSpec((1, D_MODEL), full),
            pl.BlockSpec((1, D_MODEL), full),
            pl.BlockSpec((tm, NA_WIDTH), tok),
            pl.BlockSpec((2, tm, S5_WIDTH), lambda i: (0, i, 0)),
            pl.BlockSpec((2, tm, LRU_WIDTH), lambda i: (0, i, 0)),
            pl.BlockSpec((tm, LRU_WIDTH), tok),
            wspec(w_gate), wspec(w_glu), wspec(w_ba), wspec(w_bs), wspec(w_bl), wspec(w_out),
        ],
        out_specs=[pl.BlockSpec((tm, D_MODEL), tok), pl.BlockSpec((tm, D_MODEL), tok)],
        out_shape=[jax.ShapeDtypeStruct((t_tokens, D_MODEL), F32),
                   jax.ShapeDtypeStruct((t_tokens, D_MODEL), BF16)],
        compiler_params=_params("arbitrary"),
        name="merge",
    )(x, mod6, g1, g2, attn, s5_y, lru_h, xg, w_gate, w_glu, w_ba, w_bs, w_bl, w_out)


def _ffn_kernel(h_ref, x_ref, mod_ref, wa_ref, wb_ref, cw_ref, cb_ref, wd_ref, gf_ref, o_ref,
                acc_ref, pad_ref, *, seq_len, final):
    j = pl.program_id(1)
    tm = h_ref.shape[0]
    fc = wa_ref.shape[1]
    h = h_ref[...]
    a = _dot(h, wa_ref[...])
    b = _dot(h, wb_ref[...])
    pad_ref[0:SUBLANES, :] = jnp.zeros((SUBLANES, fc), F32)
    pad_ref[SUBLANES + tm:2 * SUBLANES + tm, :] = jnp.zeros((SUBLANES, fc), F32)
    pad_ref[SUBLANES:SUBLANES + tm, :] = a
    t = lax.broadcasted_iota(jnp.int32, (tm, 1), 0) & (seq_len - 1)
    prev = jnp.where(t >= 1, pad_ref[SUBLANES - 1:SUBLANES - 1 + tm, :], 0.0)
    nxt = jnp.where(t < seq_len - 1, pad_ref[SUBLANES + 1:SUBLANES + 1 + tm, :], 0.0)
    conv = cw_ref[0:1, :] * prev + cw_ref[1:2, :] * a + cw_ref[2:3, :] * nxt + cb_ref[...]
    act = (_gelu(conv) * b).astype(BF16)
    part = _dot(act, wd_ref[...])

    @pl.when(j == 0)
    def _():
        acc_ref[...] = part

    @pl.when(j > 0)
    def _():
        acc_ref[...] += part

    @pl.when(j == pl.num_programs(1) - 1)
    def _():
        x2 = x_ref[...] + mod_ref[0][5:6] * acc_ref[...]
        if final:
            x2 = _rms(x2) * gf_ref[...]
        o_ref[...] = x2


def _ffn(h2, x1, mod6, mod_row, w_up, conv_w, conv_b, w_down, g_final, seq_len, final):
    t_tokens = x1.shape[0]
    tm = 1024
    fc = 256
    nj = FFN_DIM // fc
    tok = lambda i, j: (i, 0)
    return pl.pallas_call(
        functools.partial(_ffn_kernel, seq_len=seq_len, final=final),
        grid=(t_tokens // tm, nj),
        in_specs=[
            pl.BlockSpec((tm, D_MODEL), tok),
            pl.BlockSpec((tm, D_MODEL), tok),
            pl.BlockSpec((1, 6, D_MODEL), lambda i, j: (mod_row(i, tm), 0, 0)),
            pl.BlockSpec((D_MODEL, fc), lambda i, j: (0, j)),
            pl.BlockSpec((D_MODEL, fc), lambda i, j: (0, j + nj)),
            pl.BlockSpec((FFN_CONV, fc), lambda i, j: (0, j)),
            pl.BlockSpec((1, fc), lambda i, j: (0, j)),
            pl.BlockSpec((fc, D_MODEL), lambda i, j: (j, 0)),
            pl.BlockSpec((1, D_MODEL), lambda i, j: (0, 0)),
        ],
        out_specs=pl.BlockSpec((tm, D_MODEL), tok),
        out_shape=jax.ShapeDtypeStruct((t_tokens, D_MODEL), F32),
        scratch_shapes=[pltpu.VMEM((tm, D_MODEL), F32),
                        pltpu.VMEM((tm + 2 * SUBLANES, fc), F32)],
        compiler_params=_params("arbitrary", "arbitrary"),
        name="ffn",
    )(h2, x1, mod6, w_up, w_up, conv_w, conv_b, w_down, g_final)


def _layer(x, bsz, length, mod6, mod_row, lw, attention, s5_h0, lru_h0, kv_dtype, final):
    q, k, v, u, xc, xg = _in_proj(x, mod6, mod_row, lw['g1'], lw['w_mix'], lw['lru_conv_w'],
                                  lw['lru_conv_b'], length, kv_dtype)
    seq = lambda a: a.reshape(bsz, length, a.shape[-1])
    attn = attention(seq(q), seq(k), seq(v)).reshape(bsz * length, NA_WIDTH)
    s5_y, s5_last = _s5_scan(seq(u), lw['s5_bd'], lw['s5_cd'], lw['s5_lam_bar'], lw['s5_d'], s5_h0)
    lru_h, lru_last = _lru_scan(seq(xc), lw['lru_wg'], lw['lru_bg'], lw['lru_lam'], lru_h0)
    x1, h2 = _merge(x, mod6, mod_row, lw['g1'], lw['g2'], attn,
                    s5_y.reshape(2, bsz * length, S5_WIDTH), lru_h.reshape(2, bsz * length, LRU_WIDTH),
                    xg, lw['w_gate'], lw['s5_w_glu'], lw['w_br_attn'], lw['w_br_s5'], lw['w_br_lru'],
                    lw['w_out'])
    x2 = _ffn(h2, x1, mod6, mod_row, lw['ffn_w_up'], lw['ffn_conv_w'], lw['ffn_conv_b'],
              lw['ffn_w_down'], lw['g_final'], length, final)
    return x2, (k, v, s5_last, lru_last)


def kernel(x_prompt, x_sample, cache_k, cache_v, state_s5_re, state_s5_im, state_lru, c, c_ctx,
           w_ada, b_ada, g_norm1, g_norm2, w_in, rpb,
           s5_lam_re, s5_lam_im, s5_log_step, s5_b_re, s5_b_im, s5_c_re, s5_c_im, s5_d, s5_w_glu,
           lru_conv_w, lru_conv_b, lru_w_a, lru_b_a, lru_w_x, lru_b_x, lru_lam,
           w_br_attn, w_br_s5, w_br_lru, w_out,
           ffn_w_up, ffn_conv_w, ffn_conv_b, ffn_w_down, g_final):
    bsz, seq_len, _ = x_prompt.shape
    dec_bsz, dec_len, _ = x_sample.shape
    past = cache_k.shape[2]
    assert dec_bsz + 1 <= COND_ROWS and bsz % SUBLANES == 0 and dec_bsz % SUBLANES == 0

    cond = jnp.zeros((COND_ROWS, D_MODEL), F32).at[:dec_bsz].set(c).at[dec_bsz].set(c_ctx)
    mod = _ada(cond, w_ada, b_ada).reshape(DEPTH, COND_ROWS, 6, D_MODEL)
    ctx_row = lambda i, tm: dec_bsz
    dec_row = lambda i, tm: (i * tm) // dec_len

    cache_k = cache_k.reshape(dec_bsz, DEPTH, past, NA_WIDTH)
    cache_v = cache_v.reshape(dec_bsz, DEPTH, past, NA_WIDTH)
    yp = x_prompt.reshape(bsz * seq_len, D_MODEL)
    ys = x_sample.reshape(dec_bsz * dec_len, D_MODEL)
    ks, vs, s5s, lrus = [], [], [], []
    for l in range(DEPTH):
        lam_bar, b_bar = _s5_prep(s5_lam_re[l], s5_lam_im[l], s5_log_step[l], s5_b_re[l], s5_b_im[l])
        s5_bd, s5_cd = _s5_matrices(b_bar, s5_c_re[l], s5_c_im[l])
        lam_bar = lam_bar.transpose(1, 0, 2)
        lru_wg, lru_bg = _lru_gate_matrices(lru_w_a[l], lru_b_a[l], lru_w_x[l], lru_b_x[l])
        row = lambda a: a[l].reshape(1, -1)
        lw = {
            'g1': row(g_norm1), 'g2': row(g_norm2), 'g_final': g_final.reshape(1, D_MODEL),
            'w_mix': w_in[l, :, :MIX_WIDTH].astype(BF16), 'w_gate': w_in[l, :, MIX_WIDTH:].astype(BF16),
            's5_bd': s5_bd, 's5_cd': s5_cd, 's5_lam_bar': lam_bar, 's5_d': row(s5_d),
            's5_w_glu': s5_w_glu[l].astype(BF16),
            'lru_conv_w': lru_conv_w[l], 'lru_conv_b': row(lru_conv_b),
            'lru_wg': lru_wg, 'lru_bg': lru_bg, 'lru_lam': lru_lam[l],
            'w_br_attn': w_br_attn[l].astype(BF16), 'w_br_s5': w_br_s5[l].astype(BF16),
            'w_br_lru': w_br_lru[l].astype(BF16), 'w_out': w_out[l].astype(BF16),
            'ffn_w_up': ffn_w_up[l].astype(BF16), 'ffn_conv_w': ffn_conv_w[l],
            'ffn_conv_b': row(ffn_conv_b), 'ffn_w_down': ffn_w_down[l].astype(BF16),
        }
        final = l == DEPTH - 1
        yp, (k_l, v_l, s5_last, lru_last) = _layer(
            yp, bsz, seq_len, mod[l], ctx_row, lw, _ctx_attention,
            jnp.zeros((2, 2, bsz, S5_LANES), F32), jnp.zeros((2, bsz, LRU_WIDTH), F32), F32, final)
        ks.append(k_l.reshape(bsz, seq_len, NA_HEADS, HEAD_DIM))
        vs.append(v_l.reshape(bsz, seq_len, NA_HEADS, HEAD_DIM))
        s5s.append(s5_last)
        lrus.append(lru_last)
        tbl = _na_bias_table(rpb[l], dec_len)
        s5_h0 = jnp.stack([state_s5_re[:, l], state_s5_im[:, l]], axis=0)
        s5_h0 = s5_h0.reshape(2, dec_bsz, 2, S5_LANES).transpose(2, 0, 1, 3)
        lru_h0 = state_lru[:, l].transpose(1, 0, 2)
        na = functools.partial(_na_attention, cache_k=cache_k, cache_v=cache_v, layer=l, tbl=tbl)
        ys, _ = _layer(ys, dec_bsz, dec_len, mod[l], dec_row, lw, na, s5_h0, lru_h0, BF16, final)

    s5_all = jnp.stack(s5s, axis=0)
    s5_all = s5_all.transpose(2, 3, 0, 1, 4).reshape(2, bsz, DEPTH, 2, S5_GROUPS, S5_STATE)
    new_lru = jnp.stack(lrus, axis=0).transpose(2, 0, 1, 3)
    return (yp.reshape(bsz, seq_len, D_MODEL), ys.reshape(dec_bsz, dec_len, D_MODEL),
            jnp.stack(ks, axis=1), jnp.stack(vs, axis=1), s5_all[0], s5_all[1], new_lru)
```

```python
import functools

import jax
import jax.numpy as jnp
from jax import lax
from jax.experimental import pallas as pl
from jax.experimental.pallas import tpu as pltpu

D_MODEL = 1024
DEPTH = 2
GRID_W = 64
NA_HEADS = 8
HEAD_DIM = 64
NA_WIDTH = NA_HEADS * HEAD_DIM
NA_MAX_ROWS = 8
NA_COLS = 16
S5_GROUPS = 16
S5_GROUP_CH = 16
S5_WIDTH = S5_GROUPS * S5_GROUP_CH
S5_STATE = 64
S5_LANES = S5_GROUPS * S5_STATE
LRU_WIDTH = 256
LRU_BLOCKS = 4
LRU_BLOCK = LRU_WIDTH // LRU_BLOCKS
LRU_C = 8.0
LRU_CONV = 4
FFN_DIM = 2816
FFN_CONV = 3
N_BRANCH = 3
MIX_WIDTH = 3 * NA_WIDTH + S5_WIDTH + 2 * LRU_WIDTH
EPS = 1e-6
NEG_INF = -1e30

SUBLANES = 8
COND_ROWS = 16
VMEM_LIMIT = 56 * 1024 * 1024

F32 = jnp.float32
BF16 = jnp.bfloat16


def _params(*semantics):
    return pltpu.CompilerParams(dimension_semantics=semantics, vmem_limit_bytes=VMEM_LIMIT)


def _dot(a, b):
    return jnp.dot(a, b, preferred_element_type=F32)


def _dot_nt(a, b):
    return lax.dot_general(a, b, (((1,), (1,)), ((), ())), preferred_element_type=F32)


def _rms(x):
    return x * lax.rsqrt(jnp.mean(x * x, axis=-1, keepdims=True) + EPS)


def _gelu(x):
    return jax.nn.gelu(x)


def _ada_kernel(cond_ref, w_ref, b_ref, o_ref):
    c = cond_ref[...]
    s = c * jax.nn.sigmoid(c)
    o_ref[0] = _dot(s.astype(BF16), w_ref[0].astype(BF16)) + b_ref[0]


def _ada(cond, w_ada, b_ada):
    nb = 1536
    n = 6 * D_MODEL
    return pl.pallas_call(
        _ada_kernel,
        grid=(DEPTH, n // nb),
        in_specs=[
            pl.BlockSpec((COND_ROWS, D_MODEL), lambda l, j: (0, 0)),
            pl.BlockSpec((1, D_MODEL, nb), lambda l, j: (l, 0, j)),
            pl.BlockSpec((1, 1, nb), lambda l, j: (l, 0, j)),
        ],
        out_specs=pl.BlockSpec((1, COND_ROWS, nb), lambda l, j: (l, 0, j)),
        out_shape=jax.ShapeDtypeStruct((DEPTH, COND_ROWS, n), F32),
        compiler_params=_params("arbitrary", "arbitrary"),
        name="ada",
    )(cond, w_ada, b_ada.reshape(DEPTH, 1, n))


def _in_kernel(x_ref, mod_ref, g_ref, w_ref, cw_ref, cb_ref,
               q_ref, k_ref, v_ref, u_ref, xc_ref, xg_ref, pad_ref, *, seq_len):
    tm = x_ref.shape[0]
    md = mod_ref[0]
    h = (_rms(x_ref[...]) * g_ref[...]) * (1.0 + md[1:2]) + md[0:1]
    z = _dot(h.astype(BF16), w_ref[...])
    q_ref[...] = (z[:, 0:NA_WIDTH] * HEAD_DIM ** -0.5).astype(q_ref.dtype)
    k_ref[...] = z[:, NA_WIDTH:2 * NA_WIDTH].astype(k_ref.dtype)
    v_ref[...] = z[:, 2 * NA_WIDTH:3 * NA_WIDTH].astype(v_ref.dtype)
    o = 3 * NA_WIDTH
    u_ref[...] = z[:, o:o + S5_WIDTH]
    xg_ref[...] = z[:, o + S5_WIDTH + LRU_WIDTH:o + S5_WIDTH + 2 * LRU_WIDTH]
    xr = z[:, o + S5_WIDTH:o + S5_WIDTH + LRU_WIDTH]
    pad_ref[0:SUBLANES, :] = jnp.zeros((SUBLANES, LRU_WIDTH), F32)
    pad_ref[SUBLANES + tm:2 * SUBLANES + tm, :] = jnp.zeros((SUBLANES, LRU_WIDTH), F32)
    pad_ref[SUBLANES:SUBLANES + tm, :] = xr
    t = lax.broadcasted_iota(jnp.int32, (tm, 1), 0) & (seq_len - 1)
    acc = jnp.zeros((tm, LRU_WIDTH), F32) + cb_ref[...]
    pad_l = LRU_CONV // 2
    for kk in range(LRU_CONV):
        s = kk - pad_l
        xs = xr if s == 0 else pad_ref[SUBLANES + s:SUBLANES + s + tm, :]
        valid = jnp.logical_and(t + s >= 0, t + s < seq_len)
        acc = acc + cw_ref[kk:kk + 1, :] * jnp.where(valid, xs, 0.0)
    xc_ref[...] = acc


def _in_proj(x, mod6, mod_row, g, w_mix, conv_w, conv_b, seq_len, kv_dtype):
    t_tokens = x.shape[0]
    tm = 1024
    tok = lambda i: (i, 0)
    full = lambda i: (0, 0)
    widths = (NA_WIDTH, NA_WIDTH, NA_WIDTH, S5_WIDTH, LRU_WIDTH, LRU_WIDTH)
    dtypes = (BF16, kv_dtype, kv_dtype, F32, F32, F32)
    return pl.pallas_call(
        functools.partial(_in_kernel, seq_len=seq_len),
        grid=(t_tokens // tm,),
        in_specs=[
            pl.BlockSpec((tm, D_MODEL), tok),
            pl.BlockSpec((1, 6, D_MODEL), lambda i: (mod_row(i, tm), 0, 0)),
            pl.BlockSpec((1, D_MODEL), full),
            pl.BlockSpec((D_MODEL, MIX_WIDTH), full),
            pl.BlockSpec((LRU_CONV, LRU_WIDTH), full),
            pl.BlockSpec((1, LRU_WIDTH), full),
        ],
        out_specs=[pl.BlockSpec((tm, w), tok) for w in widths],
        out_shape=[jax.ShapeDtypeStruct((t_tokens, w), d) for w, d in zip(widths, dtypes)],
        scratch_shapes=[pltpu.VMEM((tm + 2 * SUBLANES, LRU_WIDTH), F32)],
        compiler_params=_params("arbitrary"),
        name="in_proj",
    )(x, mod6, g, w_mix, conv_w, conv_b)


def _softmax_rows(parts):
    m = parts[0].max(axis=-1, keepdims=True)
    for p in parts[1:]:
        m = jnp.maximum(m, p.max(axis=-1, keepdims=True))
    es = [jnp.exp(p - m) for p in parts]
    den = es[0].sum(axis=-1, keepdims=True)
    for e in es[1:]:
        den = den + e.sum(axis=-1, keepdims=True)
    return [e.astype(BF16) for e in es], 1.0 / den


def _head_pair_queries(q2):
    first = lax.broadcasted_iota(jnp.int32, (1, 2 * HEAD_DIM), 1) < HEAD_DIM
    zero = jnp.zeros_like(q2)
    return jnp.concatenate([jnp.where(first, q2, zero), jnp.where(first, zero, q2)], axis=0), first


def _ctx_attn_kernel(q_ref, k_ref, v_ref, o_ref, s_ref):
    length = q_ref.shape[1]
    pairs = NA_HEADS // 2
    cols = lambda j: slice(2 * HEAD_DIM * j, 2 * HEAD_DIM * (j + 1))
    first = None

    def scores(j):
        qs, is_first = _head_pair_queries(q_ref[0, :, cols(j)])
        s_ref[j % 2] = _dot_nt(qs, k_ref[0, :, cols(j)].astype(BF16))
        return is_first

    def finish(j):
        (p,), inv = _softmax_rows([s_ref[j % 2]])
        o = _dot(p, v_ref[0, :, cols(j)].astype(BF16)) * inv
        o_ref[0, :, cols(j)] = jnp.where(first, o[0:length], o[length:2 * length]).astype(o_ref.dtype)

    first = scores(0)
    for j in range(pairs):
        if j + 1 < pairs:
            scores(j + 1)
        finish(j)


def _ctx_attention(q, k, v):
    bsz, length, _ = q.shape
    spec = pl.BlockSpec((1, length, NA_WIDTH), lambda b: (b, 0, 0))
    return pl.pallas_call(
        _ctx_attn_kernel,
        grid=(bsz,),
        in_specs=[spec, spec, spec],
        out_specs=spec,
        out_shape=jax.ShapeDtypeStruct((bsz, length, NA_WIDTH), BF16),
        scratch_shapes=[pltpu.VMEM((2, 2 * length, length), F32)],
        compiler_params=_params("arbitrary"),
        name="ctx_attn",
    )(q, k, v)


def _na_rows(length):
    rows = length // GRID_W
    return rows, min(NA_MAX_ROWS, rows)


N_DY = 2 * NA_MAX_ROWS - 1
N_DX = 2 * NA_COLS - 1


def _na_build_bias(rpb_ref, bias_ref, pair):
    shape = (2 * GRID_W, 2 * GRID_W)
    row = lax.broadcasted_iota(jnp.int32, shape, 0)
    lane = lax.broadcasted_iota(jnp.int32, shape, 1)
    qc = row & (GRID_W - 1)
    kc = lane & (GRID_W - 1)
    top = row < GRID_W
    dx = jnp.clip(kc - qc + NA_COLS - 1, 0, N_DX - 1)
    win0 = jnp.clip(qc - NA_COLS // 2, 0, GRID_W - NA_COLS)
    in_win = jnp.logical_and(kc >= win0, kc < win0 + NA_COLS)

    def one(dy, carry):
        base0 = ((2 * pair) * N_DY + dy) * N_DX
        base1 = base0 + N_DY * N_DX
        val = jnp.full(shape, NEG_INF, F32)
        for j in range(N_DX):
            val = jnp.where(dx == j, jnp.where(top, rpb_ref[base0 + j], rpb_ref[base1 + j]), val)
        bias_ref[dy] = jnp.where(in_win, val, NEG_INF)
        return carry

    lax.fori_loop(0, N_DY, one, 0)


def _na_attn_kernel(rpb_ref, q_ref, k_ref, v_ref, kc_ref, vc_ref, o_ref, bias_ref, s_ref):
    rows, wr = _na_rows(q_ref.shape[1])
    nloc = wr * GRID_W

    @pl.when(pl.program_id(1) == 0)
    def _():
        _na_build_bias(rpb_ref, bias_ref, pl.program_id(0))

    left = lax.broadcasted_iota(jnp.int32, (1, 2 * GRID_W), 1) < GRID_W
    first = lax.broadcasted_iota(jnp.int32, (1, 2 * HEAD_DIM), 1) < HEAD_DIM
    start = lambda r: min(max(r - wr // 2, 0), rows - wr)

    def scores(r):
        qs, _ = _head_pair_queries(q_ref[0, r * GRID_W:(r + 1) * GRID_W, :])
        kl = k_ref[0, start(r) * GRID_W:start(r) * GRID_W + nloc, :]
        dy0 = start(r) - r + NA_MAX_ROWS - 1
        bias = jnp.concatenate(
            [jnp.where(left, bias_ref[dy0 + 2 * m], bias_ref[dy0 + 2 * m + 1]) for m in range(wr // 2)],
            axis=-1)
        s_ref[r % 2, :, 0:nloc] = _dot_nt(qs, kl) + bias
        s_ref[r % 2, :, nloc:] = _dot_nt(qs, kc_ref[0, 0].astype(BF16))

    def finish(r):
        (p_loc, p_ctx), inv = _softmax_rows([s_ref[r % 2, :, 0:nloc], s_ref[r % 2, :, nloc:]])
        vl = v_ref[0, start(r) * GRID_W:start(r) * GRID_W + nloc, :]
        o = (_dot(p_loc, vl) + _dot(p_ctx, vc_ref[0, 0].astype(BF16))) * inv
        o_ref[0, r * GRID_W:(r + 1) * GRID_W, :] = jnp.where(
            first, o[0:GRID_W], o[GRID_W:2 * GRID_W]).astype(o_ref.dtype)

    scores(0)
    for r in range(rows):
        if r + 1 < rows:
            scores(r + 1)
        finish(r)


def _na_attention(q, k, v, cache_k, cache_v, layer, rpb):
    bsz, length, _ = q.shape
    ctx = cache_k.shape[2]
    rows, wr = _na_rows(length)
    assert wr % 2 == 0 and GRID_W == HEAD_DIM
    pair = 2 * HEAD_DIM
    tok = pl.BlockSpec((1, length, pair), lambda j, b: (b, 0, j))
    cache = pl.BlockSpec((1, 1, ctx, pair), lambda j, b: (b, layer, 0, j))
    return pl.pallas_call(
        _na_attn_kernel,
        grid=(NA_HEADS // 2, bsz),
        in_specs=[pl.BlockSpec(memory_space=pltpu.SMEM), tok, tok, tok, cache, cache],
        out_specs=tok,
        out_shape=jax.ShapeDtypeStruct((bsz, length, NA_WIDTH), BF16),
        scratch_shapes=[pltpu.VMEM((N_DY, 2 * GRID_W, 2 * GRID_W), F32),
                        pltpu.VMEM((2, 2 * GRID_W, wr * GRID_W + ctx), F32)],
        compiler_params=_params("arbitrary", "arbitrary"),
        name="na_attn",
    )(rpb.reshape(-1), q, k, v, cache_k, cache_v)


LANES = 128


def _to_time_major(src_ref, dst_ref):
    tc = src_ref.shape[1]
    for b in range(SUBLANES):
        for j in range(dst_ref.shape[0]):
            dst_ref[j, pl.ds(b, tc, stride=SUBLANES), :] = src_ref[b, :, j * LANES:(j + 1) * LANES]
    return jnp.concatenate([dst_ref[j] for j in range(dst_ref.shape[0])], axis=-1)


def _from_time_major(src_ref, store):
    tc = src_ref.shape[1] // SUBLANES
    for b in range(SUBLANES):
        for j in range(src_ref.shape[0]):
            store(b, j, src_ref[j, pl.ds(b, tc, stride=SUBLANES), :])


def _s5_prep_kernel(lre_ref, lim_ref, step_ref, bre_ref, bim_ref, lam_ref, bb_ref):
    lre = lre_ref[...]
    lim = lim_ref[...]
    step = jnp.exp(step_ref[...])
    mag = jnp.exp(lre * step)
    ang = lim * step
    ar = mag * jnp.cos(ang)
    ai = mag * jnp.sin(ang)
    lam_ref[0] = ar
    lam_ref[1] = ai
    den = lre * lre + lim * lim
    kr = ((ar - 1.0) * lre + ai * lim) / den
    ki = (ai * lre - (ar - 1.0) * lim) / den
    for d in range(2):
        bb_ref[d, 0] = kr[d:d + 1] * bre_ref[...] - ki[d:d + 1] * bim_ref[...]
        bb_ref[d, 1] = kr[d:d + 1] * bim_ref[...] + ki[d:d + 1] * bre_ref[...]


def _s5_prep(lam_re, lam_im, log_step, b_re, b_im):
    step = jnp.broadcast_to(log_step[..., None], lam_re.shape).reshape(2, S5_LANES)
    bt = lambda b: b.transpose(2, 0, 1).reshape(S5_GROUP_CH, S5_LANES)
    return pl.pallas_call(
        _s5_prep_kernel,
        out_shape=[jax.ShapeDtypeStruct((2, 2, S5_LANES), F32),
                   jax.ShapeDtypeStruct((2, 2, S5_GROUP_CH, S5_LANES), F32)],
        name="s5_prep",
    )(lam_re.reshape(2, S5_LANES), lam_im.reshape(2, S5_LANES), step, bt(b_re), bt(b_im))


def _s5_matrices(b_bar, c_re, c_im):
    eye = jnp.eye(S5_GROUPS, dtype=F32)
    bb = b_bar.reshape(2, 2, S5_GROUP_CH, S5_GROUPS, S5_STATE)
    bd = jnp.einsum('dcpgn,gh->dgpchn', bb, eye)
    bd = bd.reshape(2, S5_WIDTH, 2 * S5_LANES).astype(BF16)
    cc = jnp.stack([c_re, c_im], axis=1)
    cd = jnp.einsum('dcgpn,gh->dcgnhp', cc, eye)
    cd = cd.reshape(2, 2, S5_LANES, S5_WIDTH).astype(BF16)
    return bd, cd


def _s5_kernel(u_ref, bd_ref, cd_ref, lam_ref, dvec_ref, h0_ref, y_ref, hl_ref,
               ui_ref, bu_ref, yi_ref, st_ref, *, tc):
    d = pl.program_id(1)
    c = pl.program_id(2)

    @pl.when(c == 0)
    def _():
        st_ref[...] = h0_ref[0]

    u = _to_time_major(u_ref, ui_ref)
    bu_ref[...] = _dot(u.astype(BF16), bd_ref[0])
    shape = (SUBLANES, S5_LANES)
    lr = jnp.broadcast_to(lam_ref[0, 0:1, :], shape)
    li = jnp.broadcast_to(lam_ref[0, 1:2, :], shape)

    def step(t, carry):
        hr, hi = carry
        tt = jnp.where(d == 0, t, tc - 1 - t)
        rows = pl.ds(pl.multiple_of(tt * SUBLANES, SUBLANES), SUBLANES)
        nr = (lr * hr - li * hi) + bu_ref[rows, 0:S5_LANES]
        ni = (lr * hi + li * hr) + bu_ref[rows, S5_LANES:2 * S5_LANES]
        bu_ref[rows, 0:S5_LANES] = nr
        bu_ref[rows, S5_LANES:2 * S5_LANES] = ni
        return nr, ni

    hr, hi = lax.fori_loop(0, tc, step, (st_ref[0], st_ref[1]), unroll=4)
    st_ref[0] = hr
    st_ref[1] = hi
    hl_ref[0, 0] = hr
    hl_ref[0, 1] = hi
    hb = bu_ref[...].astype(BF16)
    y = _dot(hb[:, 0:S5_LANES], cd_ref[0, 0]) - _dot(hb[:, S5_LANES:2 * S5_LANES], cd_ref[0, 1])
    skip = jnp.where(d == 0, 1.0, 0.0)
    y = y + skip * (dvec_ref[...] * u)
    for j in range(yi_ref.shape[0]):
        yi_ref[j] = y[:, j * LANES:(j + 1) * LANES]

    def store(b, j, val):
        y_ref[0, b, :, j * LANES:(j + 1) * LANES] = val

    _from_time_major(yi_ref, store)


def _s5_scan(u, bd, cd, lam_bar, dvec, h0):
    bsz, length, _ = u.shape
    tc = 64
    nc = length // tc
    chunk = lambda d, c: c + d * (nc - 1 - 2 * c)
    return pl.pallas_call(
        functools.partial(_s5_kernel, tc=tc),
        grid=(bsz // SUBLANES, 2, nc),
        in_specs=[
            pl.BlockSpec((SUBLANES, tc, S5_WIDTH), lambda g, d, c: (g, chunk(d, c), 0)),
            pl.BlockSpec((1, S5_WIDTH, 2 * S5_LANES), lambda g, d, c: (d, 0, 0)),
            pl.BlockSpec((1, 2, S5_LANES, S5_WIDTH), lambda g, d, c: (d, 0, 0, 0)),
            pl.BlockSpec((1, 2, S5_LANES), lambda g, d, c: (d, 0, 0)),
            pl.BlockSpec((1, S5_WIDTH), lambda g, d, c: (0, 0)),
            pl.BlockSpec((1, 2, SUBLANES, S5_LANES), lambda g, d, c: (d, 0, g, 0)),
        ],
        out_specs=[
            pl.BlockSpec((1, SUBLANES, tc, S5_WIDTH), lambda g, d, c: (d, g, chunk(d, c), 0)),
            pl.BlockSpec((1, 2, SUBLANES, S5_LANES), lambda g, d, c: (d, 0, g, 0)),
        ],
        out_shape=[jax.ShapeDtypeStruct((2, bsz, length, S5_WIDTH), F32),
                   jax.ShapeDtypeStruct((2, 2, bsz, S5_LANES), F32)],
        scratch_shapes=[
            pltpu.VMEM((S5_WIDTH // LANES, tc * SUBLANES, LANES), F32),
            pltpu.VMEM((tc * SUBLANES, 2 * S5_LANES), F32),
            pltpu.VMEM((S5_WIDTH // LANES, tc * SUBLANES, LANES), F32),
            pltpu.VMEM((2, SUBLANES, S5_LANES), F32),
        ],
        compiler_params=_params("arbitrary", "arbitrary", "arbitrary"),
        name="s5_scan",
    )(u, bd, cd, lam_bar, dvec, h0)


def _lru_kernel(xc_ref, wg_ref, bg_ref, lam_ref, h0_ref, h_ref, hl_ref,
                xi_ref, a_ref, b_ref, hi_ref, st_ref, *, tc):
    d = pl.program_id(1)
    c = pl.program_id(2)

    @pl.when(c == 0)
    def _():
        st_ref[...] = h0_ref[0]

    x = _to_time_major(xc_ref, xi_ref)
    gates =_dot(x.astype(BF16), wg_ref[0]) + bg_ref[0]
    r = jax.nn.sigmoid(gates[:, 0:LRU_WIDTH])
    i = jax.nn.sigmoid(gates[:, LRU_WIDTH:2 * LRU_WIDTH])
    z = -lam_ref[0]
    softplus = jnp.maximum(z, 0.0) + jnp.log1p(jnp.exp(-jnp.abs(z)))
    log_a = -LRU_C * r * softplus
    th = jnp.tanh(log_a)
    neg_expm1 = -2.0 * th / (1.0 - th)
    a_ref[...] = jnp.exp(log_a)
    b_ref[...] = jnp.sqrt(neg_expm1) * (i * x)

    def step(t, h):
        tt = jnp.where(d == 0, t, tc - 1 - t)
        rows = pl.ds(pl.multiple_of(tt * SUBLANES, SUBLANES), SUBLANES)
        h = a_ref[rows, :] * h + b_ref[rows, :]
        for j in range(hi_ref.shape[0]):
            hi_ref[j, rows, :] = h[:, j * LANES:(j + 1) * LANES]
        return h

    h = lax.fori_loop(0, tc, step, st_ref[...], unroll=8)
    st_ref[...] = h
    hl_ref[0] = h

    def store(b, j, val):
        h_ref[0, b, :, j * LANES:(j + 1) * LANES] = val

    _from_time_major(hi_ref, store)


def _lru_gate_matrices(w_a, b_a, w_x, b_x):
    eye = jnp.eye(LRU_BLOCKS, dtype=F32)
    dense = lambda w: jnp.einsum('dhij,hk->dhikj', w, eye).reshape(2, LRU_WIDTH, LRU_WIDTH)
    wg = jnp.concatenate([dense(w_a), dense(w_x)], axis=-1).astype(BF16)
    bg = jnp.concatenate([b_a, b_x], axis=-1).reshape(2, 1, 2 * LRU_WIDTH)
    return wg, bg


def _lru_scan(xc, wg, bg, lam, h0):
    bsz, length, _ = xc.shape
    tc = 128
    nc = length // tc
    chunk = lambda d, c: c + d * (nc - 1 - 2 * c)
    rows = tc * SUBLANES
    return pl.pallas_call(
        functools.partial(_lru_kernel, tc=tc),
        grid=(bsz // SUBLANES, 2, nc),
        in_specs=[
            pl.BlockSpec((SUBLANES, tc, LRU_WIDTH), lambda g, d, c: (g, chunk(d, c), 0)),
            pl.BlockSpec((1, LRU_WIDTH, 2 * LRU_WIDTH), lambda g, d, c: (d, 0, 0)),
            pl.BlockSpec((1, 1, 2 * LRU_WIDTH), lambda g, d, c: (d, 0, 0)),
            pl.BlockSpec((1, 1, LRU_WIDTH), lambda g, d, c: (d, 0, 0)),
            pl.BlockSpec((1, SUBLANES, LRU_WIDTH), lambda g, d, c: (d, g, 0)),
        ],
        out_specs=[
            pl.BlockSpec((1, SUBLANES, tc, LRU_WIDTH), lambda g, d, c: (d, g, chunk(d, c), 0)),
            pl.BlockSpec((1, SUBLANES, LRU_WIDTH), lambda g, d, c: (d, g, 0)),
        ],
        out_shape=[jax.ShapeDtypeStruct((2, bsz, length, LRU_WIDTH), F32),
                   jax.ShapeDtypeStruct((2, bsz, LRU_WIDTH), F32)],
        scratch_shapes=[
            pltpu.VMEM((LRU_WIDTH // LANES, rows, LANES), F32),
            pltpu.VMEM((rows, LRU_WIDTH), F32),
            pltpu.VMEM((rows, LRU_WIDTH), F32),
            pltpu.VMEM((LRU_WIDTH // LANES, rows, LANES), F32),
            pltpu.VMEM((SUBLANES, LRU_WIDTH), F32),
        ],
        compiler_params=_params("arbitrary", "arbitrary", "arbitrary"),
        name="lru_scan",
    )(xc, wg, bg, lam.reshape(2, 1, LRU_WIDTH), h0)


def _merge_kernel(x_ref, mod_ref, g1_ref, g2_ref, attn_ref, s5_ref, lru_ref, xg_ref,
                  wgate_ref, wglu_ref, wba_ref, wbs_ref, wbl_ref, wout_ref, x1_ref, h2_ref):
    md = mod_ref[0]
    x = x_ref[...]
    h = (_rms(x) * g1_ref[...]) * (1.0 + md[1:2]) + md[0:1]
    gates = jax.nn.sigmoid(_dot(h.astype(BF16), wgate_ref[...]))
    ys = _gelu(s5_ref[0] + s5_ref[1])
    s5_y = ys * jax.nn.sigmoid(_dot(ys.astype(BF16), wglu_ref[...]))
    lru_y = (lru_ref[0] + lru_ref[1]) * _gelu(xg_ref[...])
    merged = (gates[:, 0:D_MODEL] * _dot(attn_ref[...], wba_ref[...])
              + gates[:, D_MODEL:2 * D_MODEL] * _dot(s5_y.astype(BF16), wbs_ref[...])
              + gates[:, 2 * D_MODEL:3 * D_MODEL] * _dot(lru_y.astype(BF16), wbl_ref[...]))
    x1 = x + md[2:3] * _dot(merged.astype(BF16), wout_ref[...])
    x1_ref[...] = x1
    h2_ref[...] = ((_rms(x1) * g2_ref[...]) * (1.0 + md[4:5]) + md[3:4]).astype(h2_ref.dtype)


def _merge(x, mod6, mod_row, g1, g2, attn, s5_y, lru_h, xg, w_gate, w_glu, w_ba, w_bs, w_bl, w_out):
    t_tokens = x.shape[0]
    tm = 512
    tok = lambda i: (i, 0)
    full = lambda i: (0, 0)
    wspec = lambda w: pl.BlockSpec(w.shape, full)
    return pl.pallas_call(
        _merge_kernel,
        grid=(t_tokens // tm,),
        in_specs=[
            pl.BlockSpec((tm, D_MODEL), tok),
            pl.BlockSpec((1, 6, D_MODEL), lambda i: (mod_row(i, tm), 0, 0)),
            pl.BlockSpec((1, D_MODEL), full),
            pl.BlockSpec((1, D_MODEL), full),
            pl.BlockSpec((tm, NA_WIDTH), tok),
            pl.BlockSpec((2, tm, S5_WIDTH), lambda i: (0, i, 0)),
            pl.BlockSpec((2, tm, LRU_WIDTH), lambda i: (0, i, 0)),
            pl.BlockSpec((tm, LRU_WIDTH), tok),
            wspec(w_gate), wspec(w_glu), wspec(w_ba), wspec(w_bs), wspec(w_bl), wspec(w_out),
        ],
        out_specs=[pl.BlockSpec((tm, D_MODEL), tok), pl.BlockSpec((tm, D_MODEL), tok)],
        out_shape=[jax.ShapeDtypeStruct((t_tokens, D_MODEL), F32),
                   jax.ShapeDtypeStruct((t_tokens, D_MODEL), BF16)],
        compiler_params=_params("arbitrary"),
        name="merge",
    )(x, mod6, g1, g2, attn, s5_y, lru_h, xg, w_gate, w_glu, w_ba, w_bs, w_bl, w_out)


FFN_CHUNK = 256


def _ffn_kernel(h_ref, x_ref, mod_ref, wup_ref, cw_ref, cb_ref, wd_ref, gf_ref, o_ref,
                acc_ref, pad_ref, b_ref, *, seq_len, final):
    tm = h_ref.shape[0]
    nj = wd_ref.shape[0]
    fc = wd_ref.shape[1]
    lo = SUBLANES
    t = lax.broadcasted_iota(jnp.int32, (tm, 1), 0) & (seq_len - 1)

    def up(j, slot):
        h = h_ref[...]
        pad_ref[slot, lo:lo + tm, :] = _dot(h, wup_ref[j])
        b_ref[slot] = _dot(h, wup_ref[j + nj])

    def finish(j, slot):
        a = pad_ref[slot, lo:lo + tm, :]
        prev = pad_ref[slot, lo - 1:lo - 1 + tm, :]
        nxt = pad_ref[slot, lo + 1:lo + 1 + tm, :]
        if seq_len < tm:
            prev = jnp.where(t >= 1, prev, 0.0)
            nxt = jnp.where(t < seq_len - 1, nxt, 0.0)
        cw = cw_ref[j]
        conv = cw[0:1, :] * prev + cw[1:2, :] * a + cw[2:3, :] * nxt + cb_ref[j]
        act = (_gelu(conv) * b_ref[slot]).astype(BF16)
        acc_ref[...] += _dot(act, wd_ref[j])

    for slot in range(2):
        pad_ref[slot, 0:lo, :] = jnp.zeros((lo, fc), F32)
        pad_ref[slot, lo + tm:2 * lo + tm, :] = jnp.zeros((lo, fc), F32)
    acc_ref[...] = jnp.zeros(acc_ref.shape, F32)
    up(0, 0)

    def pair(p, carry):
        up(2 * p + 1, 1)
        finish(2 * p, 0)
        up(2 * p + 2, 0)
        finish(2 * p + 1, 1)
        return carry

    lax.fori_loop(0, (nj - 1) // 2, pair, 0)
    finish(nj - 1, 0)
    x2 = x_ref[...] + mod_ref[0][5:6] * acc_ref[...]
    if final:
        x2 = _rms(x2) * gf_ref[...]
    o_ref[...] = x2


def _ffn_weights(w_up, conv_w, conv_b, w_down):
    nj = FFN_DIM // FFN_CHUNK
    assert nj % 2 == 1
    wup = w_up.astype(BF16).reshape(D_MODEL, 2 * nj, FFN_CHUNK).transpose(1, 0, 2)
    cw = conv_w.reshape(FFN_CONV, nj, FFN_CHUNK).transpose(1, 0, 2)
    return wup, cw, conv_b.reshape(nj, 1, FFN_CHUNK), w_down.astype(BF16).reshape(nj, FFN_CHUNK, D_MODEL)


def _ffn(h2, x1, mod6, mod_row, wup, cw, cb, wd, g_final, seq_len, final):
    t_tokens = x1.shape[0]
    tm = 1024
    tok = lambda i: (i, 0)
    whole = lambda w: pl.BlockSpec(w.shape, lambda i: (0,) * w.ndim)
    return pl.pallas_call(
        functools.partial(_ffn_kernel, seq_len=seq_len, final=final),
        grid=(t_tokens // tm,),
        in_specs=[
            pl.BlockSpec((tm, D_MODEL), tok),
            pl.BlockSpec((tm, D_MODEL), tok),
            pl.BlockSpec((1, 6, D_MODEL), lambda i: (mod_row(i, tm), 0, 0)),
            whole(wup), whole(cw), whole(cb), whole(wd), whole(g_final),
        ],
        out_specs=pl.BlockSpec((tm, D_MODEL), tok),
        out_shape=jax.ShapeDtypeStruct((t_tokens, D_MODEL), F32),
        scratch_shapes=[pltpu.VMEM((tm, D_MODEL), F32),
                        pltpu.VMEM((2, tm + 2 * SUBLANES, FFN_CHUNK), F32),
                        pltpu.VMEM((2, tm, FFN_CHUNK), F32)],
        compiler_params=_params("arbitrary"),
        name="ffn",
    )(h2, x1, mod6, wup, cw, cb, wd, g_final)


def _layer(x, bsz, length, mod6, mod_row, lw, attention, s5_h0, lru_h0, kv_dtype, final):
    q, k, v, u, xc, xg = _in_proj(x, mod6, mod_row, lw['g1'], lw['w_mix'], lw['lru_conv_w'],
                                  lw['lru_conv_b'], length, kv_dtype)
    seq = lambda a: a.reshape(bsz, length, a.shape[-1])
    attn = attention(seq(q), seq(k), seq(v)).reshape(bsz * length, NA_WIDTH)
    s5_y, s5_last = _s5_scan(seq(u), lw['s5_bd'], lw['s5_cd'], lw['s5_lam_bar'], lw['s5_d'], s5_h0)
    lru_h, lru_last = _lru_scan(seq(xc), lw['lru_wg'], lw['lru_bg'], lw['lru_lam'], lru_h0)
    x1, h2 = _merge(x, mod6, mod_row, lw['g1'], lw['g2'], attn,
                    s5_y.reshape(2, bsz * length, S5_WIDTH), lru_h.reshape(2, bsz * length, LRU_WIDTH),
                    xg, lw['w_gate'], lw['s5_w_glu'], lw['w_br_attn'], lw['w_br_s5'], lw['w_br_lru'],
                    lw['w_out'])
    x2 = _ffn(h2, x1, mod6, mod_row, *lw['ffn'], lw['g_final'], length, final)
    return x2, (k, v, s5_last, lru_last)


def kernel(x_prompt, x_sample, cache_k, cache_v, state_s5_re, state_s5_im, state_lru, c, c_ctx,
           w_ada, b_ada, g_norm1, g_norm2, w_in, rpb,
           s5_lam_re, s5_lam_im, s5_log_step, s5_b_re, s5_b_im, s5_c_re, s5_c_im, s5_d, s5_w_glu,
           lru_conv_w, lru_conv_b, lru_w_a, lru_b_a, lru_w_x, lru_b_x, lru_lam,
           w_br_attn, w_br_s5, w_br_lru, w_out,
           ffn_w_up, ffn_conv_w, ffn_conv_b, ffn_w_down, g_final):
    bsz, seq_len, _ = x_prompt.shape
    dec_bsz, dec_len, _ = x_sample.shape
    past = cache_k.shape[2]
    assert dec_bsz + 1 <= COND_ROWS and bsz % SUBLANES == 0 and dec_bsz % SUBLANES == 0

    cond = jnp.zeros((COND_ROWS, D_MODEL), F32).at[:dec_bsz].set(c).at[dec_bsz].set(c_ctx)
    mod = _ada(cond, w_ada, b_ada).reshape(DEPTH, COND_ROWS, 6, D_MODEL)
    ctx_row = lambda i, tm: dec_bsz
    dec_row = lambda i, tm: (i * tm) // dec_len

    cache_k = cache_k.reshape(dec_bsz, DEPTH, past, NA_WIDTH)
    cache_v = cache_v.reshape(dec_bsz, DEPTH, past, NA_WIDTH)
    yp = x_prompt.reshape(bsz * seq_len, D_MODEL)
    ys = x_sample.reshape(dec_bsz * dec_len, D_MODEL)
    ks, vs, s5s, lrus = [], [], [], []
    for l in range(DEPTH):
        lam_bar, b_bar = _s5_prep(s5_lam_re[l], s5_lam_im[l], s5_log_step[l], s5_b_re[l], s5_b_im[l])
        s5_bd, s5_cd = _s5_matrices(b_bar, s5_c_re[l], s5_c_im[l])
        lam_bar = lam_bar.transpose(1, 0, 2)
        lru_wg, lru_bg = _lru_gate_matrices(lru_w_a[l], lru_b_a[l], lru_w_x[l], lru_b_x[l])
        row = lambda a: a[l].reshape(1, -1)
        lw = {
            'g1': row(g_norm1), 'g2': row(g_norm2), 'g_final': g_final.reshape(1, D_MODEL),
            'w_mix': w_in[l, :, :MIX_WIDTH].astype(BF16), 'w_gate': w_in[l, :, MIX_WIDTH:].astype(BF16),
            's5_bd': s5_bd, 's5_cd': s5_cd, 's5_lam_bar': lam_bar, 's5_d': row(s5_d),
            's5_w_glu': s5_w_glu[l].astype(BF16),
            'lru_conv_w': lru_conv_w[l], 'lru_conv_b': row(lru_conv_b),
            'lru_wg': lru_wg, 'lru_bg': lru_bg, 'lru_lam': lru_lam[l],
            'w_br_attn': w_br_attn[l].astype(BF16), 'w_br_s5': w_br_s5[l].astype(BF16),
            'w_br_lru': w_br_lru[l].astype(BF16), 'w_out': w_out[l].astype(BF16),
            'ffn': _ffn_weights(ffn_w_up[l], ffn_conv_w[l], ffn_conv_b[l], ffn_w_down[l]),
        }
        final = l == DEPTH - 1
        yp, (k_l, v_l, s5_last, lru_last) = _layer(
            yp, bsz, seq_len, mod[l], ctx_row, lw, _ctx_attention,
            jnp.zeros((2, 2, bsz, S5_LANES), F32), jnp.zeros((2, bsz, LRU_WIDTH), F32), F32, final)
        ks.append(k_l.reshape(bsz, seq_len, NA_HEADS, HEAD_DIM))
        vs.append(v_l.reshape(bsz, seq_len, NA_HEADS, HEAD_DIM))
        s5s.append(s5_last)
        lrus.append(lru_last)
        s5_h0 = jnp.stack([state_s5_re[:, l], state_s5_im[:, l]], axis=0)
        s5_h0 = s5_h0.reshape(2, dec_bsz, 2, S5_LANES).transpose(2, 0, 1, 3)
        lru_h0 = state_lru[:, l].transpose(1, 0, 2)
        na = functools.partial(_na_attention, cache_k=cache_k, cache_v=cache_v, layer=l, rpb=rpb[l])
        ys, _ = _layer(ys, dec_bsz, dec_len, mod[l], dec_row, lw, na, s5_h0, lru_h0, BF16, final)

    s5_all = jnp.stack(s5s, axis=0)
    s5_all = s5_all.transpose(2, 3, 0, 1, 4).reshape(2, bsz, DEPTH, 2, S5_GROUPS, S5_STATE)
    new_lru = jnp.stack(lrus, axis=0).transpose(2, 0, 1, 3)
    return (yp.reshape(bsz, seq_len, D_MODEL), ys.reshape(dec_bsz, dec_len, D_MODEL),
            jnp.stack(ks, axis=1), jnp.stack(vs, axis=1), s5_all[0], s5_all[1], new_lru)
```

```python
import functools

import jax
import jax.numpy as jnp
from jax import lax
from jax.experimental import pallas as pl
from jax.experimental.pallas import tpu as pltpu

D_MODEL = 1024
DEPTH = 2
GRID_W = 64
NA_HEADS = 8
HEAD_DIM = 64
NA_WIDTH = NA_HEADS * HEAD_DIM
NA_MAX_ROWS = 8
NA_COLS = 16
S5_GROUPS = 16
S5_GROUP_CH = 16
S5_WIDTH = S5_GROUPS * S5_GROUP_CH
S5_STATE = 64
S5_LANES = S5_GROUPS * S5_STATE
LRU_WIDTH = 256
LRU_BLOCKS = 4
LRU_BLOCK = LRU_WIDTH // LRU_BLOCKS
LRU_C = 8.0
LRU_CONV = 4
FFN_DIM = 2816
FFN_CONV = 3
N_BRANCH = 3
MIX_WIDTH = 3 * NA_WIDTH + S5_WIDTH + 2 * LRU_WIDTH
EPS = 1e-6
NEG_INF = -1e30

SUBLANES = 8
COND_ROWS = 16
VMEM_LIMIT = 56 * 1024 * 1024

F32 = jnp.float32
BF16 = jnp.bfloat16


def _params(*semantics):
    return pltpu.CompilerParams(dimension_semantics=semantics, vmem_limit_bytes=VMEM_LIMIT)


def _dot(a, b):
    return jnp.dot(a, b, preferred_element_type=F32)


def _dot_nt(a, b):
    return lax.dot_general(a, b, (((1,), (1,)), ((), ())), preferred_element_type=F32)


def _rms(x):
    return x * lax.rsqrt(jnp.mean(x * x, axis=-1, keepdims=True) + EPS)


def _gelu(x):
    return jax.nn.gelu(x)


def _ada_kernel(cond_ref, w_ref, b_ref, o_ref):
    c = cond_ref[...]
    s = c * jax.nn.sigmoid(c)
    o_ref[0] = _dot(s.astype(BF16), w_ref[0].astype(BF16)) + b_ref[0]


def _ada(cond, w_ada, b_ada):
    nb = 1536
    n = 6 * D_MODEL
    return pl.pallas_call(
        _ada_kernel,
        grid=(DEPTH, n // nb),
        in_specs=[
            pl.BlockSpec((COND_ROWS, D_MODEL), lambda l, j: (0, 0)),
            pl.BlockSpec((1, D_MODEL, nb), lambda l, j: (l, 0, j)),
            pl.BlockSpec((1, 1, nb), lambda l, j: (l, 0, j)),
        ],
        out_specs=pl.BlockSpec((1, COND_ROWS, nb), lambda l, j: (l, 0, j)),
        out_shape=jax.ShapeDtypeStruct((DEPTH, COND_ROWS, n), F32),
        compiler_params=_params("arbitrary", "arbitrary"),
        name="ada",
    )(cond, w_ada, b_ada.reshape(DEPTH, 1, n))


def _in_kernel(x_ref, mod_ref, g_ref, w_ref, wkv_ref, cw_ref, cb_ref, *rest, seq_len, kv_transposed):
    q_ref, k_ref, v_ref, u_ref, xc_ref, xg_ref, pad_ref = rest[-7:]
    tm = x_ref.shape[0]
    md = mod_ref[0]
    hb = ((_rms(x_ref[...]) * g_ref[...]) * (1.0 + md[1:2]) + md[0:1]).astype(BF16)
    z = _dot(hb, w_ref[...])
    q_ref[...] = (z[:, 0:NA_WIDTH] * HEAD_DIM ** -0.5).astype(q_ref.dtype)
    if kv_transposed:
        kv = _dot_nt(wkv_ref[...], hb)
        for s in range(tm // seq_len):
            k_ref[s, 0] = kv[0:NA_WIDTH, s * seq_len:(s + 1) * seq_len]
            v_ref[s, 0] = kv[NA_WIDTH:2 * NA_WIDTH, s * seq_len:(s + 1) * seq_len]
    else:
        kv = _dot(hb, wkv_ref[...])
        k_ref[...] = kv[:, 0:NA_WIDTH].astype(k_ref.dtype)
        v_ref[...] = kv[:, NA_WIDTH:2 * NA_WIDTH].astype(v_ref.dtype)
    o = NA_WIDTH
    u_ref[...] = z[:, o:o + S5_WIDTH]
    xg_ref[...] = z[:, o + S5_WIDTH + LRU_WIDTH:o + S5_WIDTH + 2 * LRU_WIDTH]
    xr = z[:, o + S5_WIDTH:o + S5_WIDTH + LRU_WIDTH]
    pad_ref[0:SUBLANES, :] = jnp.zeros((SUBLANES, LRU_WIDTH), F32)
    pad_ref[SUBLANES + tm:2 * SUBLANES + tm, :] = jnp.zeros((SUBLANES, LRU_WIDTH), F32)
    pad_ref[SUBLANES:SUBLANES + tm, :] = xr
    t = lax.broadcasted_iota(jnp.int32, (tm, 1), 0) & (seq_len - 1)
    acc = jnp.zeros((tm, LRU_WIDTH), F32) + cb_ref[...]
    pad_l = LRU_CONV // 2
    for kk in range(LRU_CONV):
        s = kk - pad_l
        xs = xr if s == 0 else pad_ref[SUBLANES + s:SUBLANES + s + tm, :]
        valid = jnp.logical_and(t + s >= 0, t + s < seq_len)
        acc = acc + cw_ref[kk:kk + 1, :] * jnp.where(valid, xs, 0.0)
    xc_ref[...] = acc


def _in_proj(x, mod6, mod_row, g, w_mix, w_kv, conv_w, conv_b, seq_len, kv_cache, layer):
    t_tokens = x.shape[0]
    tm = 1024
    tok = lambda i: (i, 0)
    full = lambda i: (0, 0)
    tok_spec = lambda w: pl.BlockSpec((tm, w), tok)
    tok_shape = lambda w, d: jax.ShapeDtypeStruct((t_tokens, w), d)
    kv_transposed = kv_cache is not None
    if kv_transposed:
        nseq = tm // seq_len
        kv_spec = pl.BlockSpec((nseq, 1, NA_WIDTH, seq_len), lambda i: (i, layer, 0, 0))
        kv_shape = jax.ShapeDtypeStruct((t_tokens // seq_len, DEPTH, NA_WIDTH, seq_len), F32)
        kv_cache = tuple(kv_cache)
    else:
        kv_spec, kv_shape, kv_cache = tok_spec(NA_WIDTH), tok_shape(NA_WIDTH, BF16), ()
    n_in = 7
    return pl.pallas_call(
        functools.partial(_in_kernel, seq_len=seq_len, kv_transposed=kv_transposed),
        grid=(t_tokens // tm,),
        in_specs=[
            pl.BlockSpec((tm, D_MODEL), tok),
            pl.BlockSpec((1, 6, D_MODEL), lambda i: (mod_row(i, tm), 0, 0)),
            pl.BlockSpec((1, D_MODEL), full),
            pl.BlockSpec(w_mix.shape, full),
            pl.BlockSpec(w_kv.shape, full),
            pl.BlockSpec((LRU_CONV, LRU_WIDTH), full),
            pl.BlockSpec((1, LRU_WIDTH), full),
        ] + [pl.BlockSpec(memory_space=pl.ANY) for _ in kv_cache],
        out_specs=[tok_spec(NA_WIDTH), kv_spec, kv_spec,
                   tok_spec(S5_WIDTH), tok_spec(LRU_WIDTH), tok_spec(LRU_WIDTH)],
        out_shape=[tok_shape(NA_WIDTH, BF16), kv_shape, kv_shape,
                   tok_shape(S5_WIDTH, F32), tok_shape(LRU_WIDTH, F32), tok_shape(LRU_WIDTH, F32)],
        input_output_aliases={n_in + n: 1 + n for n in range(len(kv_cache))},
        scratch_shapes=[pltpu.VMEM((tm + 2 * SUBLANES, LRU_WIDTH), F32)],
        compiler_params=_params("arbitrary"),
        name="in_proj",
    )(x, mod6, g, w_mix, w_kv, conv_w, conv_b, *kv_cache)


def _softmax_rows(parts):
    m = parts[0].max(axis=-1, keepdims=True)
    for p in parts[1:]:
        m = jnp.maximum(m, p.max(axis=-1, keepdims=True))
    es = [jnp.exp(p - m) for p in parts]
    den = es[0].sum(axis=-1, keepdims=True)
    for e in es[1:]:
        den = den + e.sum(axis=-1, keepdims=True)
    return [e.astype(BF16) for e in es], 1.0 / den


def _head_pair_queries(q2):
    first = lax.broadcasted_iota(jnp.int32, (1, 2 * HEAD_DIM), 1) < HEAD_DIM
    zero = jnp.zeros_like(q2)
    return jnp.concatenate([jnp.where(first, q2, zero), jnp.where(first, zero, q2)], axis=0), first


def _ctx_attn_kernel(q_ref, kt_ref, vt_ref, o_ref, s_ref):
    length = q_ref.shape[1]
    pairs = NA_HEADS // 2
    cols = lambda j: slice(2 * HEAD_DIM * j, 2 * HEAD_DIM * (j + 1))
    first = None

    def scores(j):
        qs, is_first = _head_pair_queries(q_ref[0, :, cols(j)])
        s_ref[j % 2] = _dot(qs, kt_ref[cols(j), :].astype(BF16))
        return is_first

    def finish(j):
        (p,), inv = _softmax_rows([s_ref[j % 2]])
        o = _dot_nt(p, vt_ref[cols(j), :].astype(BF16)) * inv
        o_ref[0, :, cols(j)] = jnp.where(first, o[0:length], o[length:2 * length]).astype(o_ref.dtype)

    first = scores(0)
    for j in range(pairs):
        if j + 1 < pairs:
            scores(j + 1)
        finish(j)


def _ctx_attention(q, kt, vt, layer):
    bsz, length, _ = q.shape
    spec = pl.BlockSpec((1, length, NA_WIDTH), lambda b: (b, 0, 0))
    tspec = pl.BlockSpec((None, None, NA_WIDTH, length), lambda b: (b, layer, 0, 0))
    return pl.pallas_call(
        _ctx_attn_kernel,
        grid=(bsz,),
        in_specs=[spec, tspec, tspec],
        out_specs=spec,
        out_shape=jax.ShapeDtypeStruct((bsz, length, NA_WIDTH), BF16),
        scratch_shapes=[pltpu.VMEM((2, 2 * length, length), F32)],
        compiler_params=_params("arbitrary"),
        name="ctx_attn",
    )(q, kt, vt)


def _na_rows(length):
    rows = length // GRID_W
    return rows, min(NA_MAX_ROWS, rows)


N_DY = 2 * NA_MAX_ROWS - 1
N_DX = 2 * NA_COLS - 1


def _na_build_bias(rpb_ref, bias_ref, pair):
    shape = (2 * GRID_W, 2 * GRID_W)
    row = lax.broadcasted_iota(jnp.int32, shape, 0)
    lane = lax.broadcasted_iota(jnp.int32, shape, 1)
    qc = row & (GRID_W - 1)
    kc = lane & (GRID_W - 1)
    top = row < GRID_W
    dx = jnp.clip(kc - qc + NA_COLS - 1, 0, N_DX - 1)
    win0 = jnp.clip(qc - NA_COLS // 2, 0, GRID_W - NA_COLS)
    in_win = jnp.logical_and(kc >= win0, kc < win0 + NA_COLS)

    def one(dy, carry):
        base0 = ((2 * pair) * N_DY + dy) * N_DX
        base1 = base0 + N_DY * N_DX
        val = jnp.full(shape, NEG_INF, F32)
        for j in range(N_DX):
            val = jnp.where(dx == j, jnp.where(top, rpb_ref[base0 + j], rpb_ref[base1 + j]), val)
        bias_ref[dy] = jnp.where(in_win, val, NEG_INF)
        return carry

    lax.fori_loop(0, N_DY, one, 0)


def _na_attn_kernel(rpb_ref, q_ref, k_ref, v_ref, kct_ref, vct_ref, o_ref, bias_ref, s_ref):
    rows, wr = _na_rows(q_ref.shape[1])
    nloc = wr * GRID_W

    @pl.when(pl.program_id(1) == 0)
    def _():
        _na_build_bias(rpb_ref, bias_ref, pl.program_id(0))

    left = lax.broadcasted_iota(jnp.int32, (1, 2 * GRID_W), 1) < GRID_W
    first = lax.broadcasted_iota(jnp.int32, (1, 2 * HEAD_DIM), 1) < HEAD_DIM
    start = lambda r: min(max(r - wr // 2, 0), rows - wr)

    def scores(r):
        qs, _ = _head_pair_queries(q_ref[0, r * GRID_W:(r + 1) * GRID_W, :])
        kl = k_ref[0, start(r) * GRID_W:start(r) * GRID_W + nloc, :]
        dy0 = start(r) - r + NA_MAX_ROWS - 1
        bias = jnp.concatenate(
            [jnp.where(left, bias_ref[dy0 + 2 * m], bias_ref[dy0 + 2 * m + 1]) for m in range(wr // 2)],
            axis=-1)
        s_ref[r % 2, :, 0:nloc] = _dot_nt(qs, kl) + bias
        s_ref[r % 2, :, nloc:] = _dot(qs, kct_ref[...].astype(BF16))

    def finish(r):
        (p_loc, p_ctx), inv = _softmax_rows([s_ref[r % 2, :, 0:nloc], s_ref[r % 2, :, nloc:]])
        vl = v_ref[0, start(r) * GRID_W:start(r) * GRID_W + nloc, :]
        o = (_dot(p_loc, vl) + _dot_nt(p_ctx, vct_ref[...].astype(BF16))) * inv
        o_ref[0, r * GRID_W:(r + 1) * GRID_W, :] = jnp.where(
            first, o[0:GRID_W], o[GRID_W:2 * GRID_W]).astype(o_ref.dtype)

    scores(0)
    for r in range(rows):
        if r + 1 < rows:
            scores(r + 1)
        finish(r)


def _na_attention(q, k, v, cache_k, cache_v, layer, rpb):
    bsz, length, _ = q.shape
    ctx = cache_k.shape[3]
    rows, wr = _na_rows(length)
    assert wr % 2 == 0 and GRID_W == HEAD_DIM
    pair = 2 * HEAD_DIM
    tok = pl.BlockSpec((1, length, pair), lambda j, b: (b, 0, j))
    cache = pl.BlockSpec((None, None, pair, ctx), lambda j, b: (b, layer, j, 0))
    return pl.pallas_call(
        _na_attn_kernel,
        grid=(NA_HEADS // 2, bsz),
        in_specs=[pl.BlockSpec(memory_space=pltpu.SMEM), tok, tok, tok, cache, cache],
        out_specs=tok,
        out_shape=jax.ShapeDtypeStruct((bsz, length, NA_WIDTH), BF16),
        scratch_shapes=[pltpu.VMEM((N_DY, 2 * GRID_W, 2 * GRID_W), F32),
                        pltpu.VMEM((2, 2 * GRID_W, wr * GRID_W + ctx), F32)],
        compiler_params=_params("arbitrary", "arbitrary"),
        name="na_attn",
    )(rpb.reshape(-1), q, k, v, cache_k, cache_v)


LANES = 128


def _to_time_major(src_ref, dst_ref):
    tc = src_ref.shape[1]
    for b in range(SUBLANES):
        for j in range(dst_ref.shape[0]):
            dst_ref[j, pl.ds(b, tc, stride=SUBLANES), :] = src_ref[b, :, j * LANES:(j + 1) * LANES]
    return jnp.concatenate([dst_ref[j] for j in range(dst_ref.shape[0])], axis=-1)


def _from_time_major(src_ref, store):
    tc = src_ref.shape[1] // SUBLANES
    for b in range(SUBLANES):
        for j in range(src_ref.shape[0]):
            store(b, j, src_ref[j, pl.ds(b, tc, stride=SUBLANES), :])


def _s5_prep_kernel(lre_ref, lim_ref, step_ref, bre_ref, bim_ref, lam_ref, bb_ref):
    lre = lre_ref[...]
    lim = lim_ref[...]
    step = jnp.exp(step_ref[...])
    mag = jnp.exp(lre * step)
    ang = lim * step
    ar = mag * jnp.cos(ang)
    ai = mag * jnp.sin(ang)
    lam_ref[0] = ar
    lam_ref[1] = ai
    den = lre * lre + lim * lim
    kr = ((ar - 1.0) * lre + ai * lim) / den
    ki = (ai * lre - (ar - 1.0) * lim) / den
    for d in range(2):
        bb_ref[d, 0] = kr[d:d + 1] * bre_ref[...] - ki[d:d + 1] * bim_ref[...]
        bb_ref[d, 1] = kr[d:d + 1] * bim_ref[...] + ki[d:d + 1] * bre_ref[...]


def _s5_prep(lam_re, lam_im, log_step, b_re, b_im):
    step = jnp.broadcast_to(log_step[..., None], lam_re.shape).reshape(2, S5_LANES)
    bt = lambda b: b.transpose(2, 0, 1).reshape(S5_GROUP_CH, S5_LANES)
    return pl.pallas_call(
        _s5_prep_kernel,
        out_shape=[jax.ShapeDtypeStruct((2, 2, S5_LANES), F32),
                   jax.ShapeDtypeStruct((2, 2, S5_GROUP_CH, S5_LANES), F32)],
        name="s5_prep",
    )(lam_re.reshape(2, S5_LANES), lam_im.reshape(2, S5_LANES), step, bt(b_re), bt(b_im))


def _s5_matrices(b_bar, c_re, c_im):
    eye = jnp.eye(S5_GROUPS, dtype=F32)
    bb = b_bar.reshape(2, 2, S5_GROUP_CH, S5_GROUPS, S5_STATE)
    bd = jnp.einsum('dcpgn,gh->dgpchn', bb, eye)
    bd = bd.reshape(2, S5_WIDTH, 2 * S5_LANES).astype(BF16)
    cc = jnp.stack([c_re, c_im], axis=1)
    cd = jnp.einsum('dcgpn,gh->dcgnhp', cc, eye)
    cd = cd.reshape(2, 2, S5_LANES, S5_WIDTH).astype(BF16)
    return bd, cd


def _s5_kernel(u_ref, bd_ref, cd_ref, lam_ref, dvec_ref, h0_ref, y_ref, hl_ref,
               ui_ref, bu_ref, yi_ref, st_ref, *, tc):
    d = pl.program_id(1)
    c = pl.program_id(2)

    @pl.when(c == 0)
    def _():
        st_ref[...] = h0_ref[0]

    _to_time_major(u_ref, ui_ref)
    shape = (SUBLANES, S5_LANES)
    lr = jnp.broadcast_to(lam_ref[0, 0:1, :], shape)
    li = jnp.broadcast_to(lam_ref[0, 1:2, :], shape)
    half = tc // 2
    re = slice(0, S5_LANES)
    im = slice(S5_LANES, 2 * S5_LANES)

    def run(reverse):
        halves = (1, 0) if reverse else (0, 1)
        rows = lambda hf: slice(hf * half * SUBLANES, (hf + 1) * half * SUBLANES)

        def u_rows(hf):
            return jnp.concatenate([ui_ref[j, rows(hf), :] for j in range(ui_ref.shape[0])], axis=-1)

        def project(hf):
            bu_ref[rows(hf), :] = _dot(u_rows(hf).astype(BF16), bd_ref[0])

        def scan(hf, hr, hi):
            steps = range(half - 1, -1, -1) if reverse else range(half)
            for t in steps:
                r = slice((hf * half + t) * SUBLANES, (hf * half + t + 1) * SUBLANES)
                nr = (lr * hr - li * hi) + bu_ref[r, re]
                ni = (lr * hi + li * hr) + bu_ref[r, im]
                bu_ref[r, re] = nr
                bu_ref[r, im] = ni
                hr, hi = nr, ni
            return hr, hi

        def readout(hf):
            hb = bu_ref[rows(hf), :].astype(BF16)
            y = _dot(hb[:, re], cd_ref[0, 0]) - _dot(hb[:, im], cd_ref[0, 1])
            if not reverse:
                y = y + dvec_ref[...] * u_rows(hf)
            for j in range(yi_ref.shape[0]):
                yi_ref[j, rows(hf), :] = y[:, j * LANES:(j + 1) * LANES]

        project(halves[0])
        project(halves[1])
        hr, hi = scan(halves[0], st_ref[0], st_ref[1])
        readout(halves[0])
        hr, hi = scan(halves[1], hr, hi)
        readout(halves[1])
        st_ref[0] = hr
        st_ref[1] = hi
        hl_ref[0, 0] = hr
        hl_ref[0, 1] = hi

    @pl.when(d == 0)
    def _():
        run(False)

    @pl.when(d == 1)
    def _():
        run(True)

    def store(b, j, val):
        y_ref[0, b, :, j * LANES:(j + 1) * LANES] = val

    _from_time_major(yi_ref, store)


def _s5_scan(u, bd, cd, lam_bar, dvec, h0):
    bsz, length, _ = u.shape
    tc = 128
    nc = length // tc
    chunk = lambda d, c: c + d * (nc - 1 - 2 * c)
    return pl.pallas_call(
        functools.partial(_s5_kernel, tc=tc),
        grid=(bsz // SUBLANES, 2, nc),
        in_specs=[
            pl.BlockSpec((SUBLANES, tc, S5_WIDTH), lambda g, d, c: (g, chunk(d, c), 0)),
            pl.BlockSpec((1, S5_WIDTH, 2 * S5_LANES), lambda g, d, c: (d, 0, 0)),
            pl.BlockSpec((1, 2, S5_LANES, S5_WIDTH), lambda g, d, c: (d, 0, 0, 0)),
            pl.BlockSpec((1, 2, S5_LANES), lambda g, d, c: (d, 0, 0)),
            pl.BlockSpec((1, S5_WIDTH), lambda g, d, c: (0, 0)),
            pl.BlockSpec((1, 2, SUBLANES, S5_LANES), lambda g, d, c: (d, 0, g, 0)),
        ],
        out_specs=[
            pl.BlockSpec((1, SUBLANES, tc, S5_WIDTH), lambda g, d, c: (d, g, chunk(d, c), 0)),
            pl.BlockSpec((1, 2, SUBLANES, S5_LANES), lambda g, d, c: (d, 0, g, 0)),
        ],
        out_shape=[jax.ShapeDtypeStruct((2, bsz, length, S5_WIDTH), F32),
                   jax.ShapeDtypeStruct((2, 2, bsz, S5_LANES), F32)],
        scratch_shapes=[
            pltpu.VMEM((S5_WIDTH // LANES, tc * SUBLANES, LANES), F32),
            pltpu.VMEM((tc * SUBLANES, 2 * S5_LANES), F32),
            pltpu.VMEM((S5_WIDTH // LANES, tc * SUBLANES, LANES), F32),
            pltpu.VMEM((2, SUBLANES, S5_LANES), F32),
        ],
        compiler_params=_params("arbitrary", "arbitrary", "arbitrary"),
        name="s5_scan",
    )(u, bd, cd, lam_bar, dvec, h0)


def _lru_kernel(xc_ref, wg_ref, bg_ref, lam_ref, h0_ref, h_ref, hl_ref,
                xi_ref, a_ref, b_ref, hi_ref, st_ref, *, tc):
    d = pl.program_id(1)
    c = pl.program_id(2)

    @pl.when(c == 0)
    def _():
        st_ref[...] = h0_ref[0]

    x = _to_time_major(xc_ref, xi_ref)
    gates =_dot(x.astype(BF16), wg_ref[0]) + bg_ref[0]
    sigmoid = lambda g: 0.5 * jnp.tanh(0.5 * g) + 0.5
    r = sigmoid(gates[:, 0:LRU_WIDTH])
    i = sigmoid(gates[:, LRU_WIDTH:2 * LRU_WIDTH])
    z = -lam_ref[0]
    softplus = jnp.maximum(z, 0.0) + jnp.log1p(jnp.exp(-jnp.abs(z)))
    log_a = -LRU_C * r * softplus
    th = jnp.tanh(log_a)
    neg_expm1 = -2.0 * th / (1.0 - th)
    a_ref[...] = jnp.exp(log_a)
    b_ref[...] = jnp.sqrt(neg_expm1) * (i * x)

    def step(t, h):
        tt = jnp.where(d == 0, t, tc - 1 - t)
        rows = pl.ds(pl.multiple_of(tt * SUBLANES, SUBLANES), SUBLANES)
        h = a_ref[rows, :] * h + b_ref[rows, :]
        for j in range(hi_ref.shape[0]):
            hi_ref[j, rows, :] = h[:, j * LANES:(j + 1) * LANES]
        return h

    h = lax.fori_loop(0, tc, step, st_ref[...], unroll=8)
    st_ref[...] = h
    hl_ref[0] = h

    def store(b, j, val):
        h_ref[0, b, :, j * LANES:(j + 1) * LANES] = val

    _from_time_major(hi_ref, store)


def _lru_gate_matrices(w_a, b_a, w_x, b_x):
    eye = jnp.eye(LRU_BLOCKS, dtype=F32)
    dense = lambda w: jnp.einsum('dhij,hk->dhikj', w, eye).reshape(2, LRU_WIDTH, LRU_WIDTH)
    wg = jnp.concatenate([dense(w_a), dense(w_x)], axis=-1).astype(BF16)
    bg = jnp.concatenate([b_a, b_x], axis=-1).reshape(2, 1, 2 * LRU_WIDTH)
    return wg, bg


def _lru_scan(xc, wg, bg, lam, h0):
    bsz, length, _ = xc.shape
    tc = 128
    nc = length // tc
    chunk = lambda d, c: c + d * (nc - 1 - 2 * c)
    rows = tc * SUBLANES
    return pl.pallas_call(
        functools.partial(_lru_kernel, tc=tc),
        grid=(bsz // SUBLANES, 2, nc),
        in_specs=[
            pl.BlockSpec((SUBLANES, tc, LRU_WIDTH), lambda g, d, c: (g, chunk(d, c), 0)),
            pl.BlockSpec((1, LRU_WIDTH, 2 * LRU_WIDTH), lambda g, d, c: (d, 0, 0)),
            pl.BlockSpec((1, 1, 2 * LRU_WIDTH), lambda g, d, c: (d, 0, 0)),
            pl.BlockSpec((1, 1, LRU_WIDTH), lambda g, d, c: (d, 0, 0)),
            pl.BlockSpec((1, SUBLANES, LRU_WIDTH), lambda g, d, c: (d, g, 0)),
        ],
        out_specs=[
            pl.BlockSpec((1, SUBLANES, tc, LRU_WIDTH), lambda g, d, c: (d, g, chunk(d, c), 0)),
            pl.BlockSpec((1, SUBLANES, LRU_WIDTH), lambda g, d, c: (d, g, 0)),
        ],
        out_shape=[jax.ShapeDtypeStruct((2, bsz, length, LRU_WIDTH), F32),
                   jax.ShapeDtypeStruct((2, bsz, LRU_WIDTH), F32)],
        scratch_shapes=[
            pltpu.VMEM((LRU_WIDTH // LANES, rows, LANES), F32),
            pltpu.VMEM((rows, LRU_WIDTH), F32),
            pltpu.VMEM((rows, LRU_WIDTH), F32),
            pltpu.VMEM((LRU_WIDTH // LANES, rows, LANES), F32),
            pltpu.VMEM((SUBLANES, LRU_WIDTH), F32),
        ],
        compiler_params=_params("arbitrary", "arbitrary", "arbitrary"),
        name="lru_scan",
    )(xc, wg, bg, lam.reshape(2, 1, LRU_WIDTH), h0)


def _merge_kernel(x_ref, mod_ref, g1_ref, g2_ref, attn_ref, s5_ref, lru_ref, xg_ref,
                  wgate_ref, wglu_ref, wba_ref, wbs_ref, wbl_ref, wout_ref, x1_ref, h2_ref):
    md = mod_ref[0]
    x = x_ref[...]
    h = (_rms(x) * g1_ref[...]) * (1.0 + md[1:2]) + md[0:1]
    gates = jax.nn.sigmoid(_dot(h.astype(BF16), wgate_ref[...]))
    ys = _gelu(s5_ref[0] + s5_ref[1])
    s5_y = ys * jax.nn.sigmoid(_dot(ys.astype(BF16), wglu_ref[...]))
    lru_y = (lru_ref[0] + lru_ref[1]) * _gelu(xg_ref[...])
    merged = (gates[:, 0:D_MODEL] * _dot(attn_ref[...], wba_ref[...])
              + gates[:, D_MODEL:2 * D_MODEL] * _dot(s5_y.astype(BF16), wbs_ref[...])
              + gates[:, 2 * D_MODEL:3 * D_MODEL] * _dot(lru_y.astype(BF16), wbl_ref[...]))
    x1 = x + md[2:3] * _dot(merged.astype(BF16), wout_ref[...])
    x1_ref[...] = x1
    h2_ref[...] = ((_rms(x1) * g2_ref[...]) * (1.0 + md[4:5]) + md[3:4]).astype(h2_ref.dtype)


def _merge(x, mod6, mod_row, g1, g2, attn, s5_y, lru_h, xg, w_gate, w_glu, w_ba, w_bs, w_bl, w_out):
    t_tokens = x.shape[0]
    tm = 512
    tok = lambda i: (i, 0)
    full = lambda i: (0, 0)
    wspec = lambda w: pl.BlockSpec(w.shape, full)
    return pl.pallas_call(
        _merge_kernel,
        grid=(t_tokens // tm,),
        in_specs=[
            pl.BlockSpec((tm, D_MODEL), tok),
            pl.BlockSpec((1, 6, D_MODEL), lambda i: (mod_row(i, tm), 0, 0)),
            pl.BlockSpec((1, D_MODEL), full),
            pl.BlockSpec((1, D_MODEL), full),
            pl.BlockSpec((tm, NA_WIDTH), tok),
            pl.BlockSpec((2, tm, S5_WIDTH), lambda i: (0, i, 0)),
            pl.BlockSpec((2, tm, LRU_WIDTH), lambda i: (0, i, 0)),
            pl.BlockSpec((tm, LRU_WIDTH), tok),
            wspec(w_gate), wspec(w_glu), wspec(w_ba), wspec(w_bs), wspec(w_bl), wspec(w_out),
        ],
        out_specs=[pl.BlockSpec((tm, D_MODEL), tok), pl.BlockSpec((tm, D_MODEL), tok)],
        out_shape=[jax.ShapeDtypeStruct((t_tokens, D_MODEL), F32),
                   jax.ShapeDtypeStruct((t_tokens, D_MODEL), BF16)],
        compiler_params=_params("arbitrary"),
        name="merge",
    )(x, mod6, g1, g2, attn, s5_y, lru_h, xg, w_gate, w_glu, w_ba, w_bs, w_bl, w_out)


FFN_CHUNK = 256


def _ffn_kernel(h_ref, x_ref, mod_ref, wup_ref, cw_ref, cb_ref, wd_ref, gf_ref, o_ref,
                acc_ref, pad_ref, b_ref, *, seq_len, final):
    tm = h_ref.shape[0]
    fc = FFN_CHUNK
    nj = FFN_DIM // fc
    lo = SUBLANES
    t = lax.broadcasted_iota(jnp.int32, (tm, 1), 0) & (seq_len - 1)
    cols = lambda j: slice(j * fc, (j + 1) * fc)

    def up(j):
        h = h_ref[...]
        pad_ref[j % 2, lo:lo + tm, :] = _dot(h, wup_ref[:, cols(j)])
        b_ref[j % 2] = _dot(h, wup_ref[:, cols(j + nj)])

    def finish(j):
        slot = j % 2
        a = pad_ref[slot, lo:lo + tm, :]
        prev = pad_ref[slot, lo - 1:lo - 1 + tm, :]
        nxt = pad_ref[slot, lo + 1:lo + 1 + tm, :]
        if seq_len < tm:
            prev = jnp.where(t >= 1, prev, 0.0)
            nxt = jnp.where(t < seq_len - 1, nxt, 0.0)
        cw = cw_ref[:, cols(j)]
        conv = cw[0:1, :] * prev + cw[1:2, :] * a + cw[2:3, :] * nxt + cb_ref[:, cols(j)]
        act = (_gelu(conv) * b_ref[slot]).astype(BF16)
        part = _dot(act, wd_ref[cols(j), :])
        if j == 0:
            acc_ref[...] = part
        else:
            acc_ref[...] += part

    for slot in range(2):
        pad_ref[slot, 0:lo, :] = jnp.zeros((lo, fc), F32)
        pad_ref[slot, lo + tm:2 * lo + tm, :] = jnp.zeros((lo, fc), F32)
    up(0)
    for j in range(nj):
        if j + 1 < nj:
            up(j + 1)
        finish(j)
    x2 = x_ref[...] + mod_ref[0][5:6] * acc_ref[...]
    if final:
        x2 = _rms(x2) * gf_ref[...]
    o_ref[...] = x2


def _ffn(h2, x1, mod6, mod_row, wup, cw, cb, wd, g_final, seq_len, final):
    t_tokens = x1.shape[0]
    tm = 1024
    tok = lambda i: (i, 0)
    whole = lambda w: pl.BlockSpec(w.shape, lambda i: (0,) * w.ndim)
    return pl.pallas_call(
        functools.partial(_ffn_kernel, seq_len=seq_len, final=final),
        grid=(t_tokens // tm,),
        in_specs=[
            pl.BlockSpec((tm, D_MODEL), tok),
            pl.BlockSpec((tm, D_MODEL), tok),
            pl.BlockSpec((1, 6, D_MODEL), lambda i: (mod_row(i, tm), 0, 0)),
            whole(wup), whole(cw), whole(cb), whole(wd), whole(g_final),
        ],
        out_specs=pl.BlockSpec((tm, D_MODEL), tok),
        out_shape=jax.ShapeDtypeStruct((t_tokens, D_MODEL), F32),
        scratch_shapes=[pltpu.VMEM((tm, D_MODEL), F32),
                        pltpu.VMEM((2, tm + 2 * SUBLANES, FFN_CHUNK), F32),
                        pltpu.VMEM((2, tm, FFN_CHUNK), F32)],
        compiler_params=_params("arbitrary"),
        name="ffn",
    )(h2, x1, mod6, wup, cw, cb, wd, g_final)


def _layer(x, bsz, length, mod6, mod_row, lw, attention, s5_h0, lru_h0, kv_cache, layer, final):
    q, k, v, u, xc, xg = _in_proj(x, mod6, mod_row, lw['g1'], lw['w_mix'],
                                  lw['w_kv'] if kv_cache is None else lw['w_kv_t'],
                                  lw['lru_conv_w'], lw['lru_conv_b'], length, kv_cache, layer)
    seq = lambda a: a.reshape(bsz, length, a.shape[-1])
    kv = seq if kv_cache is None else (lambda a: a)
    attn = attention(seq(q), kv(k), kv(v)).reshape(bsz * length, NA_WIDTH)
    s5_y, s5_last = _s5_scan(seq(u), lw['s5_bd'], lw['s5_cd'], lw['s5_lam_bar'], lw['s5_d'], s5_h0)
    lru_h, lru_last = _lru_scan(seq(xc), lw['lru_wg'], lw['lru_bg'], lw['lru_lam'], lru_h0)
    x1, h2 = _merge(x, mod6, mod_row, lw['g1'], lw['g2'], attn,
                    s5_y.reshape(2, bsz * length, S5_WIDTH), lru_h.reshape(2, bsz * length, LRU_WIDTH),
                    xg, lw['w_gate'], lw['s5_w_glu'], lw['w_br_attn'], lw['w_br_s5'], lw['w_br_lru'],
                    lw['w_out'])
    x2 = _ffn(h2, x1, mod6, mod_row, *lw['ffn'], lw['g_final'], length, final)
    return x2, (k, v, s5_last, lru_last)


def kernel(x_prompt, x_sample, cache_k, cache_v, state_s5_re, state_s5_im, state_lru, c, c_ctx,
           w_ada, b_ada, g_norm1, g_norm2, w_in, rpb,
           s5_lam_re, s5_lam_im, s5_log_step, s5_b_re, s5_b_im, s5_c_re, s5_c_im, s5_d, s5_w_glu,
           lru_conv_w, lru_conv_b, lru_w_a, lru_b_a, lru_w_x, lru_b_x, lru_lam,
           w_br_attn, w_br_s5, w_br_lru, w_out,
           ffn_w_up, ffn_conv_w, ffn_conv_b, ffn_w_down, g_final):
    bsz, seq_len, _ = x_prompt.shape
    dec_bsz, dec_len, _ = x_sample.shape
    past = cache_k.shape[2]
    assert dec_bsz + 1 <= COND_ROWS and bsz % SUBLANES == 0 and dec_bsz % SUBLANES == 0

    cond = jnp.zeros((COND_ROWS, D_MODEL), F32).at[:dec_bsz].set(c).at[dec_bsz].set(c_ctx)
    mod = _ada(cond, w_ada, b_ada).reshape(DEPTH, COND_ROWS, 6, D_MODEL)
    ctx_row = lambda i, tm: dec_bsz
    dec_row = lambda i, tm: (i * tm) // dec_len

    cache_t = lambda a: a.transpose(0, 1, 3, 4, 2).reshape(dec_bsz, DEPTH, NA_WIDTH, past)
    cache_k, cache_v = cache_t(cache_k), cache_t(cache_v)
    yp = x_prompt.reshape(bsz * seq_len, D_MODEL)
    ys = x_sample.reshape(dec_bsz * dec_len, D_MODEL)
    s5s, lrus = [], []
    kv_caches = ()
    for l in range(DEPTH):
        lam_bar, b_bar = _s5_prep(s5_lam_re[l], s5_lam_im[l], s5_log_step[l], s5_b_re[l], s5_b_im[l])
        s5_bd, s5_cd = _s5_matrices(b_bar, s5_c_re[l], s5_c_im[l])
        lam_bar = lam_bar.transpose(1, 0, 2)
        lru_wg, lru_bg = _lru_gate_matrices(lru_w_a[l], lru_b_a[l], lru_w_x[l], lru_b_x[l])
        row = lambda a: a[l].reshape(1, -1)
        lw = {
            'g1': row(g_norm1), 'g2': row(g_norm2), 'g_final': g_final.reshape(1, D_MODEL),
            'w_mix': jnp.concatenate([w_in[l, :, :NA_WIDTH], w_in[l, :, 3 * NA_WIDTH:MIX_WIDTH]],
                                     axis=1).astype(BF16),
            'w_kv': w_in[l, :, NA_WIDTH:3 * NA_WIDTH].astype(BF16),
            'w_kv_t': w_in[l, :, NA_WIDTH:3 * NA_WIDTH].T.astype(BF16),
            'w_gate': w_in[l, :, MIX_WIDTH:].astype(BF16),
            's5_bd': s5_bd, 's5_cd': s5_cd, 's5_lam_bar': lam_bar, 's5_d': row(s5_d),
            's5_w_glu': s5_w_glu[l].astype(BF16),
            'lru_conv_w': lru_conv_w[l], 'lru_conv_b': row(lru_conv_b),
            'lru_wg': lru_wg, 'lru_bg': lru_bg, 'lru_lam': lru_lam[l],
            'w_br_attn': w_br_attn[l].astype(BF16), 'w_br_s5': w_br_s5[l].astype(BF16),
            'w_br_lru': w_br_lru[l].astype(BF16), 'w_out': w_out[l].astype(BF16),
            'ffn': (ffn_w_up[l].astype(BF16), ffn_conv_w[l], row(ffn_conv_b), ffn_w_down[l].astype(BF16)),
        }
        final = l == DEPTH - 1
        yp, (new_kt, new_vt, s5_last, lru_last) = _layer(
            yp, bsz, seq_len, mod[l], ctx_row, lw, functools.partial(_ctx_attention, layer=l),
            jnp.zeros((2, 2, bsz, S5_LANES), F32), jnp.zeros((2, bsz, LRU_WIDTH), F32),
            kv_caches, l, final)
        kv_caches = (new_kt, new_vt)
        s5s.append(s5_last)
        lrus.append(lru_last)
        s5_h0 = jnp.stack([state_s5_re[:, l], state_s5_im[:, l]], axis=0)
        s5_h0 = s5_h0.reshape(2, dec_bsz, 2, S5_LANES).transpose(2, 0, 1, 3)
        lru_h0 = state_lru[:, l].transpose(1, 0, 2)
        na = functools.partial(_na_attention, cache_k=cache_k, cache_v=cache_v, layer=l, rpb=rpb[l])
        ys, _ = _layer(ys, dec_bsz, dec_len, mod[l], dec_row, lw, na, s5_h0, lru_h0, None, l, final)

    s5_all = jnp.stack(s5s, axis=0)
    s5_all = s5_all.transpose(2, 3, 0, 1, 4).reshape(2, bsz, DEPTH, 2, S5_GROUPS, S5_STATE)
    new_lru = jnp.stack(lrus, axis=0).transpose(2, 0, 1, 3)
    cache_out = lambda a: a.reshape(bsz, DEPTH, NA_HEADS, HEAD_DIM, seq_len).transpose(0, 1, 4, 2, 3)
    return (yp.reshape(bsz, seq_len, D_MODEL), ys.reshape(dec_bsz, dec_len, D_MODEL),
            cache_out(kv_caches[0]), cache_out(kv_caches[1]), s5_all[0], s5_all[1], new_lru)
```

```python
import functools

import jax
import jax.numpy as jnp
from jax import lax
from jax.experimental import pallas as pl
from jax.experimental.pallas import tpu as pltpu

D_MODEL = 1024
DEPTH = 2
GRID_W = 64
NA_HEADS = 8
HEAD_DIM = 64
NA_WIDTH = NA_HEADS * HEAD_DIM
NA_MAX_ROWS = 8
NA_COLS = 16
S5_GROUPS = 16
S5_GROUP_CH = 16
S5_WIDTH = S5_GROUPS * S5_GROUP_CH
S5_STATE = 64
S5_LANES = S5_GROUPS * S5_STATE
LRU_WIDTH = 256
LRU_BLOCKS = 4
LRU_BLOCK = LRU_WIDTH // LRU_BLOCKS
LRU_C = 8.0
LRU_CONV = 4
FFN_DIM = 2816
FFN_CONV = 3
N_BRANCH = 3
MIX_WIDTH = 3 * NA_WIDTH + S5_WIDTH + 2 * LRU_WIDTH
EPS = 1e-6
NEG_INF = -1e30

SUBLANES = 8
COND_ROWS = 16
VMEM_LIMIT = 56 * 1024 * 1024

F32 = jnp.float32
BF16 = jnp.bfloat16


def _params(*semantics):
    return pltpu.CompilerParams(dimension_semantics=semantics, vmem_limit_bytes=VMEM_LIMIT)


def _dot(a, b):
    return jnp.dot(a, b, preferred_element_type=F32)


def _dot_nt(a, b):
    return lax.dot_general(a, b, (((1,), (1,)), ((), ())), preferred_element_type=F32)


def _rms(x):
    return x * lax.rsqrt(jnp.mean(x * x, axis=-1, keepdims=True) + EPS)


def _gelu(x):
    return jax.nn.gelu(x)


def _ada_kernel(cond_ref, w_ref, b_ref, o_ref):
    c = cond_ref[...]
    s = c * jax.nn.sigmoid(c)
    o_ref[0] = _dot(s.astype(BF16), w_ref[0].astype(BF16)) + b_ref[0]


def _ada(cond, w_ada, b_ada):
    nb = 1536
    n = 6 * D_MODEL
    return pl.pallas_call(
        _ada_kernel,
        grid=(DEPTH, n // nb),
        in_specs=[
            pl.BlockSpec((COND_ROWS, D_MODEL), lambda l, j: (0, 0)),
            pl.BlockSpec((1, D_MODEL, nb), lambda l, j: (l, 0, j)),
            pl.BlockSpec((1, 1, nb), lambda l, j: (l, 0, j)),
        ],
        out_specs=pl.BlockSpec((1, COND_ROWS, nb), lambda l, j: (l, 0, j)),
        out_shape=jax.ShapeDtypeStruct((DEPTH, COND_ROWS, n), F32),
        compiler_params=_params("arbitrary", "arbitrary"),
        name="ada",
    )(cond, w_ada, b_ada.reshape(DEPTH, 1, n))


def _in_kernel(x_ref, mod_ref, g_ref, w_ref, wkv_ref, cw_ref, cb_ref, *rest, seq_len, kv_transposed):
    q_ref, k_ref, v_ref, u_ref, xc_ref, xg_ref, pad_ref = rest[-7:]
    tm = x_ref.shape[0]
    md = mod_ref[0]
    hb = ((_rms(x_ref[...]) * g_ref[...]) * (1.0 + md[1:2]) + md[0:1]).astype(BF16)
    z = _dot(hb, w_ref[...])
    q_ref[...] = (z[:, 0:NA_WIDTH] * HEAD_DIM ** -0.5).astype(q_ref.dtype)
    if kv_transposed:
        kv = _dot_nt(wkv_ref[...], hb)
        for s in range(tm // seq_len):
            k_ref[s, 0] = kv[0:NA_WIDTH, s * seq_len:(s + 1) * seq_len]
            v_ref[s, 0] = kv[NA_WIDTH:2 * NA_WIDTH, s * seq_len:(s + 1) * seq_len]
    else:
        kv = _dot(hb, wkv_ref[...])
        k_ref[...] = kv[:, 0:NA_WIDTH].astype(k_ref.dtype)
        v_ref[...] = kv[:, NA_WIDTH:2 * NA_WIDTH].astype(v_ref.dtype)
    o = NA_WIDTH
    u_ref[...] = z[:, o:o + S5_WIDTH]
    xg_ref[...] = z[:, o + S5_WIDTH + LRU_WIDTH:o + S5_WIDTH + 2 * LRU_WIDTH]
    xr = z[:, o + S5_WIDTH:o + S5_WIDTH + LRU_WIDTH]
    pad_ref[0:SUBLANES, :] = jnp.zeros((SUBLANES, LRU_WIDTH), F32)
    pad_ref[SUBLANES + tm:2 * SUBLANES + tm, :] = jnp.zeros((SUBLANES, LRU_WIDTH), F32)
    pad_ref[SUBLANES:SUBLANES + tm, :] = xr
    t = lax.broadcasted_iota(jnp.int32, (tm, 1), 0) & (seq_len - 1)
    acc = jnp.zeros((tm, LRU_WIDTH), F32) + cb_ref[...]
    pad_l = LRU_CONV // 2
    for kk in range(LRU_CONV):
        s = kk - pad_l
        xs = xr if s == 0 else pad_ref[SUBLANES + s:SUBLANES + s + tm, :]
        valid = jnp.logical_and(t + s >= 0, t + s < seq_len)
        acc = acc + cw_ref[kk:kk + 1, :] * jnp.where(valid, xs, 0.0)
    xc_ref[...] = acc


def _in_proj(x, mod6, mod_row, g, w_mix, w_kv, conv_w, conv_b, seq_len, kv_cache, layer):
    t_tokens = x.shape[0]
    tm = 1024
    tok = lambda i: (i, 0)
    full = lambda i: (0, 0)
    tok_spec = lambda w: pl.BlockSpec((tm, w), tok)
    tok_shape = lambda w, d: jax.ShapeDtypeStruct((t_tokens, w), d)
    kv_transposed = kv_cache is not None
    if kv_transposed:
        nseq = tm // seq_len
        kv_spec = pl.BlockSpec((nseq, 1, NA_WIDTH, seq_len), lambda i: (i, layer, 0, 0))
        kv_shape = jax.ShapeDtypeStruct((t_tokens // seq_len, DEPTH, NA_WIDTH, seq_len), F32)
        kv_cache = tuple(kv_cache)
    else:
        kv_spec, kv_shape, kv_cache = tok_spec(NA_WIDTH), tok_shape(NA_WIDTH, BF16), ()
    n_in = 7
    return pl.pallas_call(
        functools.partial(_in_kernel, seq_len=seq_len, kv_transposed=kv_transposed),
        grid=(t_tokens // tm,),
        in_specs=[
            pl.BlockSpec((tm, D_MODEL), tok),
            pl.BlockSpec((1, 6, D_MODEL), lambda i: (mod_row(i, tm), 0, 0)),
            pl.BlockSpec((1, D_MODEL), full),
            pl.BlockSpec(w_mix.shape, full),
            pl.BlockSpec(w_kv.shape, full),
            pl.BlockSpec((LRU_CONV, LRU_WIDTH), full),
            pl.BlockSpec((1, LRU_WIDTH), full),
        ] + [pl.BlockSpec(memory_space=pl.ANY) for _ in kv_cache],
        out_specs=[tok_spec(NA_WIDTH), kv_spec, kv_spec,
                   tok_spec(S5_WIDTH), tok_spec(LRU_WIDTH), tok_spec(LRU_WIDTH)],
        out_shape=[tok_shape(NA_WIDTH, BF16), kv_shape, kv_shape,
                   tok_shape(S5_WIDTH, F32), tok_shape(LRU_WIDTH, F32), tok_shape(LRU_WIDTH, F32)],
        input_output_aliases={n_in + n: 1 + n for n in range(len(kv_cache))},
        scratch_shapes=[pltpu.VMEM((tm + 2 * SUBLANES, LRU_WIDTH), F32)],
        compiler_params=_params("arbitrary"),
        name="in_proj",
    )(x, mod6, g, w_mix, w_kv, conv_w, conv_b, *kv_cache)


def _softmax_rows(parts):
    m = parts[0].max(axis=-1, keepdims=True)
    for p in parts[1:]:
        m = jnp.maximum(m, p.max(axis=-1, keepdims=True))
    es = [jnp.exp(p - m) for p in parts]
    den = es[0].sum(axis=-1, keepdims=True)
    for e in es[1:]:
        den = den + e.sum(axis=-1, keepdims=True)
    return [e.astype(BF16) for e in es], 1.0 / den


def _head_pair_queries(q2):
    first = lax.broadcasted_iota(jnp.int32, (1, 2 * HEAD_DIM), 1) < HEAD_DIM
    zero = jnp.zeros_like(q2)
    return jnp.concatenate([jnp.where(first, q2, zero), jnp.where(first, zero, q2)], axis=0), first


def _ctx_attn_kernel(q_ref, kt_ref, vt_ref, o_ref, s_ref):
    length = q_ref.shape[1]
    pairs = NA_HEADS // 2
    cols = lambda j: slice(2 * HEAD_DIM * j, 2 * HEAD_DIM * (j + 1))
    first = None

    def scores(j):
        qs, is_first = _head_pair_queries(q_ref[0, :, cols(j)])
        s_ref[j % 2] = _dot(qs, kt_ref[cols(j), :].astype(BF16))
        return is_first

    def finish(j):
        (p,), inv = _softmax_rows([s_ref[j % 2]])
        o = _dot_nt(p, vt_ref[cols(j), :].astype(BF16)) * inv
        o_ref[0, :, cols(j)] = jnp.where(first, o[0:length], o[length:2 * length]).astype(o_ref.dtype)

    first = scores(0)
    for j in range(pairs):
        if j + 1 < pairs:
            scores(j + 1)
        finish(j)


def _ctx_attention(q, kt, vt, layer):
    bsz, length, _ = q.shape
    spec = pl.BlockSpec((1, length, NA_WIDTH), lambda b: (b, 0, 0))
    tspec = pl.BlockSpec((None, None, NA_WIDTH, length), lambda b: (b, layer, 0, 0))
    return pl.pallas_call(
        _ctx_attn_kernel,
        grid=(bsz,),
        in_specs=[spec, tspec, tspec],
        out_specs=spec,
        out_shape=jax.ShapeDtypeStruct((bsz, length, NA_WIDTH), BF16),
        scratch_shapes=[pltpu.VMEM((2, 2 * length, length), F32)],
        compiler_params=_params("arbitrary"),
        name="ctx_attn",
    )(q, kt, vt)


def _na_rows(length):
    rows = length // GRID_W
    return rows, min(NA_MAX_ROWS, rows)


N_DY = 2 * NA_MAX_ROWS - 1
N_DX = 2 * NA_COLS - 1


def _na_build_bias(rpb_ref, bias_ref, pair):
    shape = (2 * GRID_W, 2 * GRID_W)
    row = lax.broadcasted_iota(jnp.int32, shape, 0)
    lane = lax.broadcasted_iota(jnp.int32, shape, 1)
    qc = row & (GRID_W - 1)
    kc = lane & (GRID_W - 1)
    top = row < GRID_W
    dx = jnp.clip(kc - qc + NA_COLS - 1, 0, N_DX - 1)
    win0 = jnp.clip(qc - NA_COLS // 2, 0, GRID_W - NA_COLS)
    in_win = jnp.logical_and(kc >= win0, kc < win0 + NA_COLS)

    def one(dy, carry):
        base0 = ((2 * pair) * N_DY + dy) * N_DX
        base1 = base0 + N_DY * N_DX
        val = jnp.full(shape, NEG_INF, F32)
        for j in range(N_DX):
            val = jnp.where(dx == j, jnp.where(top, rpb_ref[base0 + j], rpb_ref[base1 + j]), val)
        bias_ref[dy] = jnp.where(in_win, val, NEG_INF)
        return carry

    lax.fori_loop(0, N_DY, one, 0)

    left = lane < GRID_W

    def pair_up(dy, carry):
        bias_ref[dy] = jnp.where(left, bias_ref[dy], bias_ref[dy + 1])
        return carry

    lax.fori_loop(0, N_DY - 1, pair_up, 0)


def _na_attn_kernel(rpb_ref, q_ref, k_ref, v_ref, kct_ref, vct_ref, o_ref,
                    bias_ref, s_ref, kcb_ref, vcb_ref):
    rows, wr = _na_rows(q_ref.shape[1])
    nloc = wr * GRID_W

    @pl.when(pl.program_id(1) == 0)
    def _():
        _na_build_bias(rpb_ref, bias_ref, pl.program_id(0))

    kcb_ref[...] = kct_ref[...].astype(BF16)
    vcb_ref[...] = vct_ref[...].astype(BF16)
    first = lax.broadcasted_iota(jnp.int32, (1, 2 * HEAD_DIM), 1) < HEAD_DIM
    start = lambda r: min(max(r - wr // 2, 0), rows - wr)

    def scores(r):
        qs, _ = _head_pair_queries(q_ref[0, r * GRID_W:(r + 1) * GRID_W, :])
        kl = k_ref[0, start(r) * GRID_W:start(r) * GRID_W + nloc, :]
        dy0 = start(r) - r + NA_MAX_ROWS - 1
        bias = jnp.concatenate([bias_ref[dy0 + 2 * m] for m in range(wr // 2)], axis=-1)
        s_ref[r % 2, :, 0:nloc] = _dot_nt(qs, kl) + bias
        s_ref[r % 2, :, nloc:] = _dot(qs, kcb_ref[...])

    def finish(r):
        (p_loc, p_ctx), inv = _softmax_rows([s_ref[r % 2, :, 0:nloc], s_ref[r % 2, :, nloc:]])
        vl = v_ref[0, start(r) * GRID_W:start(r) * GRID_W + nloc, :]
        o = (_dot(p_loc, vl) + _dot_nt(p_ctx, vcb_ref[...])) * inv
        o_ref[0, r * GRID_W:(r + 1) * GRID_W, :] = jnp.where(
            first, o[0:GRID_W], o[GRID_W:2 * GRID_W]).astype(o_ref.dtype)

    scores(0)
    for r in range(rows):
        if r + 1 < rows:
            scores(r + 1)
        finish(r)


def _na_attention(q, k, v, cache_k, cache_v, layer, rpb):
    bsz, length, _ = q.shape
    ctx = cache_k.shape[3]
    rows, wr = _na_rows(length)
    assert wr % 2 == 0 and GRID_W == HEAD_DIM
    pair = 2 * HEAD_DIM
    tok = pl.BlockSpec((1, length, pair), lambda j, b: (b, 0, j))
    cache = pl.BlockSpec((None, None, pair, ctx), lambda j, b: (b, layer, j, 0))
    return pl.pallas_call(
        _na_attn_kernel,
        grid=(NA_HEADS // 2, bsz),
        in_specs=[pl.BlockSpec(memory_space=pltpu.SMEM), tok, tok, tok, cache, cache],
        out_specs=tok,
        out_shape=jax.ShapeDtypeStruct((bsz, length, NA_WIDTH), BF16),
        scratch_shapes=[pltpu.VMEM((N_DY, 2 * GRID_W, 2 * GRID_W), F32),
                        pltpu.VMEM((2, 2 * GRID_W, wr * GRID_W + ctx), F32),
                        pltpu.VMEM((pair, ctx), BF16), pltpu.VMEM((pair, ctx), BF16)],
        compiler_params=_params("arbitrary", "arbitrary"),
        name="na_attn",
    )(rpb.reshape(-1), q, k, v, cache_k, cache_v)


LANES = 128


def _to_time_major(src_ref, dst_ref):
    tc = src_ref.shape[1]
    for b in range(SUBLANES):
        for j in range(dst_ref.shape[0]):
            dst_ref[j, pl.ds(b, tc, stride=SUBLANES), :] = src_ref[b, :, j * LANES:(j + 1) * LANES]
    return jnp.concatenate([dst_ref[j] for j in range(dst_ref.shape[0])], axis=-1)


def _from_time_major(src_ref, store):
    tc = src_ref.shape[1] // SUBLANES
    for b in range(SUBLANES):
        for j in range(src_ref.shape[0]):
            store(b, j, src_ref[j, pl.ds(b, tc, stride=SUBLANES), :])


def _s5_prep_kernel(lre_ref, lim_ref, step_ref, bre_ref, bim_ref, lam_ref, bb_ref):
    lre = lre_ref[...]
    lim = lim_ref[...]
    step = jnp.exp(step_ref[...])
    mag = jnp.exp(lre * step)
    ang = lim * step
    ar = mag * jnp.cos(ang)
    ai = mag * jnp.sin(ang)
    lam_ref[0] = ar
    lam_ref[1] = ai
    den = lre * lre + lim * lim
    kr = ((ar - 1.0) * lre + ai * lim) / den
    ki = (ai * lre - (ar - 1.0) * lim) / den
    for d in range(2):
        bb_ref[d, 0] = kr[d:d + 1] * bre_ref[...] - ki[d:d + 1] * bim_ref[...]
        bb_ref[d, 1] = kr[d:d + 1] * bim_ref[...] + ki[d:d + 1] * bre_ref[...]


def _s5_prep(lam_re, lam_im, log_step, b_re, b_im):
    step = jnp.broadcast_to(log_step[..., None], lam_re.shape).reshape(2, S5_LANES)
    bt = lambda b: b.transpose(2, 0, 1).reshape(S5_GROUP_CH, S5_LANES)
    return pl.pallas_call(
        _s5_prep_kernel,
        out_shape=[jax.ShapeDtypeStruct((2, 2, S5_LANES), F32),
                   jax.ShapeDtypeStruct((2, 2, S5_GROUP_CH, S5_LANES), F32)],
        name="s5_prep",
    )(lam_re.reshape(2, S5_LANES), lam_im.reshape(2, S5_LANES), step, bt(b_re), bt(b_im))


def _s5_matrices(b_bar, c_re, c_im):
    eye = jnp.eye(S5_GROUPS, dtype=F32)
    bb = b_bar.reshape(2, 2, S5_GROUP_CH, S5_GROUPS, S5_STATE)
    bd = jnp.einsum('dcpgn,gh->dgpchn', bb, eye)
    bd = bd.reshape(2, S5_WIDTH, 2 * S5_LANES).astype(BF16)
    cc = jnp.stack([c_re, c_im], axis=1)
    cd = jnp.einsum('dcgpn,gh->dcgnhp', cc, eye)
    cd = cd.reshape(2, 2, S5_LANES, S5_WIDTH).astype(BF16)
    return bd, cd


def _s5_kernel(u_ref, bd_ref, cd_ref, lam_ref, dvec_ref, h0_ref, y_ref, hl_ref,
               ui_ref, bu_ref, yi_ref, st_ref, *, tc):
    d = pl.program_id(1)
    c = pl.program_id(2)

    @pl.when(c == 0)
    def _():
        st_ref[...] = h0_ref[0]

    _to_time_major(u_ref, ui_ref)
    shape = (SUBLANES, S5_LANES)
    lr = jnp.broadcast_to(lam_ref[0, 0:1, :], shape)
    li = jnp.broadcast_to(lam_ref[0, 1:2, :], shape)
    half = tc // 2
    re = slice(0, S5_LANES)
    im = slice(S5_LANES, 2 * S5_LANES)

    def run(reverse):
        halves = (1, 0) if reverse else (0, 1)
        rows = lambda hf: slice(hf * half * SUBLANES, (hf + 1) * half * SUBLANES)

        def u_rows(hf):
            return jnp.concatenate([ui_ref[j, rows(hf), :] for j in range(ui_ref.shape[0])], axis=-1)

        def project(hf):
            bu_ref[rows(hf), :] = _dot(u_rows(hf).astype(BF16), bd_ref[0])

        def scan(hf, hr, hi):
            steps = range(half - 1, -1, -1) if reverse else range(half)
            for t in steps:
                r = slice((hf * half + t) * SUBLANES, (hf * half + t + 1) * SUBLANES)
                nr = (lr * hr - li * hi) + bu_ref[r, re]
                ni = (lr * hi + li * hr) + bu_ref[r, im]
                bu_ref[r, re] = nr
                bu_ref[r, im] = ni
                hr, hi = nr, ni
            return hr, hi

        def readout(hf):
            hb = bu_ref[rows(hf), :].astype(BF16)
            y = _dot(hb[:, re], cd_ref[0, 0]) - _dot(hb[:, im], cd_ref[0, 1])
            if not reverse:
                y = y + dvec_ref[...] * u_rows(hf)
            for j in range(yi_ref.shape[0]):
                yi_ref[j, rows(hf), :] = y[:, j * LANES:(j + 1) * LANES]

        project(halves[0])
        project(halves[1])
        hr, hi = scan(halves[0], st_ref[0], st_ref[1])
        readout(halves[0])
        hr, hi = scan(halves[1], hr, hi)
        readout(halves[1])
        st_ref[0] = hr
        st_ref[1] = hi
        hl_ref[0, 0] = hr
        hl_ref[0, 1] = hi

    @pl.when(d == 0)
    def _():
        run(False)

    @pl.when(d == 1)
    def _():
        run(True)

    def store(b, j, val):
        y_ref[0, b, :, j * LANES:(j + 1) * LANES] = val

    _from_time_major(yi_ref, store)


def _s5_scan(u, bd, cd, lam_bar, dvec, h0):
    bsz, length, _ = u.shape
    tc = 128
    nc = length // tc
    chunk = lambda d, c: c + d * (nc - 1 - 2 * c)
    return pl.pallas_call(
        functools.partial(_s5_kernel, tc=tc),
        grid=(bsz // SUBLANES, 2, nc),
        in_specs=[
            pl.BlockSpec((SUBLANES, tc, S5_WIDTH), lambda g, d, c: (g, chunk(d, c), 0)),
            pl.BlockSpec((1, S5_WIDTH, 2 * S5_LANES), lambda g, d, c: (d, 0, 0)),
            pl.BlockSpec((1, 2, S5_LANES, S5_WIDTH), lambda g, d, c: (d, 0, 0, 0)),
            pl.BlockSpec((1, 2, S5_LANES), lambda g, d, c: (d, 0, 0)),
            pl.BlockSpec((1, S5_WIDTH), lambda g, d, c: (0, 0)),
            pl.BlockSpec((1, 2, SUBLANES, S5_LANES), lambda g, d, c: (d, 0, g, 0)),
        ],
        out_specs=[
            pl.BlockSpec((1, SUBLANES, tc, S5_WIDTH), lambda g, d, c: (d, g, chunk(d, c), 0)),
            pl.BlockSpec((1, 2, SUBLANES, S5_LANES), lambda g, d, c: (d, 0, g, 0)),
        ],
        out_shape=[jax.ShapeDtypeStruct((2, bsz, length, S5_WIDTH), F32),
                   jax.ShapeDtypeStruct((2, 2, bsz, S5_LANES), F32)],
        scratch_shapes=[
            pltpu.VMEM((S5_WIDTH // LANES, tc * SUBLANES, LANES), F32),
            pltpu.VMEM((tc * SUBLANES, 2 * S5_LANES), F32),
            pltpu.VMEM((S5_WIDTH // LANES, tc * SUBLANES, LANES), F32),
            pltpu.VMEM((2, SUBLANES, S5_LANES), F32),
        ],
        compiler_params=_params("arbitrary", "arbitrary", "arbitrary"),
        name="s5_scan",
    )(u, bd, cd, lam_bar, dvec, h0)


def _lru_kernel(xc_ref, wg_ref, bg_ref, lam_ref, h0_ref, h_ref, hl_ref,
                xi_ref, a_ref, b_ref, hi_ref, st_ref, *, tc):
    d = pl.program_id(1)
    c = pl.program_id(2)

    @pl.when(c == 0)
    def _():
        st_ref[...] = h0_ref[0]

    x = _to_time_major(xc_ref, xi_ref)
    gates =_dot(x.astype(BF16), wg_ref[0]) + bg_ref[0]
    sigmoid = lambda g: 0.5 * jnp.tanh(0.5 * g) + 0.5
    r = sigmoid(gates[:, 0:LRU_WIDTH])
    i = sigmoid(gates[:, LRU_WIDTH:2 * LRU_WIDTH])
    z = -lam_ref[0]
    softplus = jnp.maximum(z, 0.0) + jnp.log1p(jnp.exp(-jnp.abs(z)))
    log_a = -LRU_C * r * softplus
    th = jnp.tanh(log_a)
    neg_expm1 = -2.0 * th / (1.0 - th)
    a_ref[...] = jnp.exp(log_a)
    b_ref[...] = jnp.sqrt(neg_expm1) * (i * x)

    def step(t, h):
        tt = jnp.where(d == 0, t, tc - 1 - t)
        rows = pl.ds(pl.multiple_of(tt * SUBLANES, SUBLANES), SUBLANES)
        h = a_ref[rows, :] * h + b_ref[rows, :]
        for j in range(hi_ref.shape[0]):
            hi_ref[j, rows, :] = h[:, j * LANES:(j + 1) * LANES]
        return h

    h = lax.fori_loop(0, tc, step, st_ref[...], unroll=8)
    st_ref[...] = h
    hl_ref[0] = h

    def store(b, j, val):
        h_ref[0, b, :, j * LANES:(j + 1) * LANES] = val

    _from_time_major(hi_ref, store)


def _lru_gate_matrices(w_a, b_a, w_x, b_x):
    eye = jnp.eye(LRU_BLOCKS, dtype=F32)
    dense = lambda w: jnp.einsum('dhij,hk->dhikj', w, eye).reshape(2, LRU_WIDTH, LRU_WIDTH)
    wg = jnp.concatenate([dense(w_a), dense(w_x)], axis=-1).astype(BF16)
    bg = jnp.concatenate([b_a, b_x], axis=-1).reshape(2, 1, 2 * LRU_WIDTH)
    return wg, bg


def _lru_scan(xc, wg, bg, lam, h0):
    bsz, length, _ = xc.shape
    tc = 128
    nc = length // tc
    chunk = lambda d, c: c + d * (nc - 1 - 2 * c)
    rows = tc * SUBLANES
    return pl.pallas_call(
        functools.partial(_lru_kernel, tc=tc),
        grid=(bsz // SUBLANES, 2, nc),
        in_specs=[
            pl.BlockSpec((SUBLANES, tc, LRU_WIDTH), lambda g, d, c: (g, chunk(d, c), 0)),
            pl.BlockSpec((1, LRU_WIDTH, 2 * LRU_WIDTH), lambda g, d, c: (d, 0, 0)),
            pl.BlockSpec((1, 1, 2 * LRU_WIDTH), lambda g, d, c: (d, 0, 0)),
            pl.BlockSpec((1, 1, LRU_WIDTH), lambda g, d, c: (d, 0, 0)),
            pl.BlockSpec((1, SUBLANES, LRU_WIDTH), lambda g, d, c: (d, g, 0)),
        ],
        out_specs=[
            pl.BlockSpec((1, SUBLANES, tc, LRU_WIDTH), lambda g, d, c: (d, g, chunk(d, c), 0)),
            pl.BlockSpec((1, SUBLANES, LRU_WIDTH), lambda g, d, c: (d, g, 0)),
        ],
        out_shape=[jax.ShapeDtypeStruct((2, bsz, length, LRU_WIDTH), F32),
                   jax.ShapeDtypeStruct((2, bsz, LRU_WIDTH), F32)],
        scratch_shapes=[
            pltpu.VMEM((LRU_WIDTH // LANES, rows, LANES), F32),
            pltpu.VMEM((rows, LRU_WIDTH), F32),
            pltpu.VMEM((rows, LRU_WIDTH), F32),
            pltpu.VMEM((LRU_WIDTH // LANES, rows, LANES), F32),
            pltpu.VMEM((SUBLANES, LRU_WIDTH), F32),
        ],
        compiler_params=_params("arbitrary", "arbitrary", "arbitrary"),
        name="lru_scan",
    )(xc, wg, bg, lam.reshape(2, 1, LRU_WIDTH), h0)


def _merge_kernel(x_ref, mod_ref, g1_ref, g2_ref, attn_ref, s5_ref, lru_ref, xg_ref,
                  wgate_ref, wglu_ref, wba_ref, wbs_ref, wbl_ref, wout_ref, x1_ref, h2_ref):
    md = mod_ref[0]
    x = x_ref[...]
    h = (_rms(x) * g1_ref[...]) * (1.0 + md[1:2]) + md[0:1]
    gates = jax.nn.sigmoid(_dot(h.astype(BF16), wgate_ref[...]))
    ys = _gelu(s5_ref[0] + s5_ref[1])
    s5_y = ys * jax.nn.sigmoid(_dot(ys.astype(BF16), wglu_ref[...]))
    lru_y = (lru_ref[0] + lru_ref[1]) * _gelu(xg_ref[...])
    merged = (gates[:, 0:D_MODEL] * _dot(attn_ref[...], wba_ref[...])
              + gates[:, D_MODEL:2 * D_MODEL] * _dot(s5_y.astype(BF16), wbs_ref[...])
              + gates[:, 2 * D_MODEL:3 * D_MODEL] * _dot(lru_y.astype(BF16), wbl_ref[...]))
    x1 = x + md[2:3] * _dot(merged.astype(BF16), wout_ref[...])
    x1_ref[...] = x1
    h2_ref[...] = ((_rms(x1) * g2_ref[...]) * (1.0 + md[4:5]) + md[3:4]).astype(h2_ref.dtype)


def _merge(x, mod6, mod_row, g1, g2, attn, s5_y, lru_h, xg, w_gate, w_glu, w_ba, w_bs, w_bl, w_out):
    t_tokens = x.shape[0]
    tm = 512
    tok = lambda i: (i, 0)
    full = lambda i: (0, 0)
    wspec = lambda w: pl.BlockSpec(w.shape, full)
    return pl.pallas_call(
        _merge_kernel,
        grid=(t_tokens // tm,),
        in_specs=[
            pl.BlockSpec((tm, D_MODEL), tok),
            pl.BlockSpec((1, 6, D_MODEL), lambda i: (mod_row(i, tm), 0, 0)),
            pl.BlockSpec((1, D_MODEL), full),
            pl.BlockSpec((1, D_MODEL), full),
            pl.BlockSpec((tm, NA_WIDTH), tok),
            pl.BlockSpec((2, tm, S5_WIDTH), lambda i: (0, i, 0)),
            pl.BlockSpec((2, tm, LRU_WIDTH), lambda i: (0, i, 0)),
            pl.BlockSpec((tm, LRU_WIDTH), tok),
            wspec(w_gate), wspec(w_glu), wspec(w_ba), wspec(w_bs), wspec(w_bl), wspec(w_out),
        ],
        out_specs=[pl.BlockSpec((tm, D_MODEL), tok), pl.BlockSpec((tm, D_MODEL), tok)],
        out_shape=[jax.ShapeDtypeStruct((t_tokens, D_MODEL), F32),
                   jax.ShapeDtypeStruct((t_tokens, D_MODEL), BF16)],
        compiler_params=_params("arbitrary"),
        name="merge",
    )(x, mod6, g1, g2, attn, s5_y, lru_h, xg, w_gate, w_glu, w_ba, w_bs, w_bl, w_out)


FFN_CHUNK = 256
FFN_ROWS = 256


def _ffn_kernel(h_ref, x_ref, mod_ref, wup_ref, cw_ref, cb_ref, wd_ref, gf_ref, o_ref,
                acc_ref, pad_ref, b_ref, *, seq_len, final):
    tm = h_ref.shape[0]
    fc = FFN_CHUNK
    nj = FFN_DIM // fc
    lo = SUBLANES
    t = lax.broadcasted_iota(jnp.int32, (tm, 1), 0) & (seq_len - 1)
    cols = lambda j: slice(j * fc, (j + 1) * fc)
    rb = FFN_ROWS
    rows = lambda i: slice(i * rb, (i + 1) * rb)

    def up(j, i):
        h = h_ref[rows(i), :]
        pad_ref[j % 2, lo + i * rb:lo + (i + 1) * rb, :] = _dot(h, wup_ref[:, cols(j)])
        b_ref[j % 2, rows(i), :] = _dot(h, wup_ref[:, cols(j + nj)])

    def finish(j, i):
        slot = j % 2
        r0 = lo + i * rb
        a = pad_ref[slot, r0:r0 + rb, :]
        prev = pad_ref[slot, r0 - 1:r0 - 1 + rb, :]
        nxt = pad_ref[slot, r0 + 1:r0 + 1 + rb, :]
        prev = jnp.where(t[rows(i)] >= 1, prev, 0.0)
        nxt = jnp.where(t[rows(i)] < seq_len - 1, nxt, 0.0)
        cw = cw_ref[:, cols(j)]
        conv = cw[0:1, :] * prev + cw[1:2, :] * a + cw[2:3, :] * nxt + cb_ref[:, cols(j)]
        act = (_gelu(conv) * b_ref[slot, rows(i), :]).astype(BF16)
        part = _dot(act, wd_ref[cols(j), :])
        if j == 0:
            acc_ref[rows(i), :] = part
        else:
            acc_ref[rows(i), :] += part

    for slot in range(2):
        pad_ref[slot, 0:lo, :] = jnp.zeros((lo, fc), F32)
        pad_ref[slot, lo + tm:2 * lo + tm, :] = jnp.zeros((lo, fc), F32)
    for i in range(tm // rb):
        up(0, i)
    for j in range(nj):
        for i in range(tm // rb):
            if j + 1 < nj:
                up(j + 1, i)
            finish(j, i)
    x2 = x_ref[...] + mod_ref[0][5:6] * acc_ref[...]
    if final:
        x2 = _rms(x2) * gf_ref[...]
    o_ref[...] = x2


def _ffn(h2, x1, mod6, mod_row, wup, cw, cb, wd, g_final, seq_len, final):
    t_tokens = x1.shape[0]
    tm = 1024
    tok = lambda i: (i, 0)
    whole = lambda w: pl.BlockSpec(w.shape, lambda i: (0,) * w.ndim)
    return pl.pallas_call(
        functools.partial(_ffn_kernel, seq_len=seq_len, final=final),
        grid=(t_tokens // tm,),
        in_specs=[
            pl.BlockSpec((tm, D_MODEL), tok),
            pl.BlockSpec((tm, D_MODEL), tok),
            pl.BlockSpec((1, 6, D_MODEL), lambda i: (mod_row(i, tm), 0, 0)),
            whole(wup), whole(cw), whole(cb), whole(wd), whole(g_final),
        ],
        out_specs=pl.BlockSpec((tm, D_MODEL), tok),
        out_shape=jax.ShapeDtypeStruct((t_tokens, D_MODEL), F32),
        scratch_shapes=[pltpu.VMEM((tm, D_MODEL), F32),
                        pltpu.VMEM((2, tm + 2 * SUBLANES, FFN_CHUNK), F32),
                        pltpu.VMEM((2, tm, FFN_CHUNK), F32)],
        compiler_params=_params("arbitrary"),
        name="ffn",
    )(h2, x1, mod6, wup, cw, cb, wd, g_final)


def _layer(x, bsz, length, mod6, mod_row, lw, attention, s5_h0, lru_h0, kv_cache, layer, final):
    q, k, v, u, xc, xg = _in_proj(x, mod6, mod_row, lw['g1'], lw['w_mix'],
                                  lw['w_kv'] if kv_cache is None else lw['w_kv_t'],
                                  lw['lru_conv_w'], lw['lru_conv_b'], length, kv_cache, layer)
    seq = lambda a: a.reshape(bsz, length, a.shape[-1])
    kv = seq if kv_cache is None else (lambda a: a)
    attn = attention(seq(q), kv(k), kv(v)).reshape(bsz * length, NA_WIDTH)
    s5_y, s5_last = _s5_scan(seq(u), lw['s5_bd'], lw['s5_cd'], lw['s5_lam_bar'], lw['s5_d'], s5_h0)
    lru_h, lru_last = _lru_scan(seq(xc), lw['lru_wg'], lw['lru_bg'], lw['lru_lam'], lru_h0)
    x1, h2 = _merge(x, mod6, mod_row, lw['g1'], lw['g2'], attn,
                    s5_y.reshape(2, bsz * length, S5_WIDTH), lru_h.reshape(2, bsz * length, LRU_WIDTH),
                    xg, lw['w_gate'], lw['s5_w_glu'], lw['w_br_attn'], lw['w_br_s5'], lw['w_br_lru'],
                    lw['w_out'])
    x2 = _ffn(h2, x1, mod6, mod_row, *lw['ffn'], lw['g_final'], length, final)
    return x2, (k, v, s5_last, lru_last)


def kernel(x_prompt, x_sample, cache_k, cache_v, state_s5_re, state_s5_im, state_lru, c, c_ctx,
           w_ada, b_ada, g_norm1, g_norm2, w_in, rpb,
           s5_lam_re, s5_lam_im, s5_log_step, s5_b_re, s5_b_im, s5_c_re, s5_c_im, s5_d, s5_w_glu,
           lru_conv_w, lru_conv_b, lru_w_a, lru_b_a, lru_w_x, lru_b_x, lru_lam,
           w_br_attn, w_br_s5, w_br_lru, w_out,
           ffn_w_up, ffn_conv_w, ffn_conv_b, ffn_w_down, g_final):
    bsz, seq_len, _ = x_prompt.shape
    dec_bsz, dec_len, _ = x_sample.shape
    past = cache_k.shape[2]
    assert dec_bsz + 1 <= COND_ROWS and bsz % SUBLANES == 0 and dec_bsz % SUBLANES == 0

    cond = jnp.zeros((COND_ROWS, D_MODEL), F32).at[:dec_bsz].set(c).at[dec_bsz].set(c_ctx)
    mod = _ada(cond, w_ada, b_ada).reshape(DEPTH, COND_ROWS, 6, D_MODEL)
    ctx_row = lambda i, tm: dec_bsz
    dec_row = lambda i, tm: (i * tm) // dec_len

    cache_t = lambda a: a.transpose(0, 1, 3, 4, 2).reshape(dec_bsz, DEPTH, NA_WIDTH, past)
    cache_k, cache_v = cache_t(cache_k), cache_t(cache_v)
    yp = x_prompt.reshape(bsz * seq_len, D_MODEL)
    ys = x_sample.reshape(dec_bsz * dec_len, D_MODEL)
    s5s, lrus = [], []
    kv_caches = ()
    for l in range(DEPTH):
        lam_bar, b_bar = _s5_prep(s5_lam_re[l], s5_lam_im[l], s5_log_step[l], s5_b_re[l], s5_b_im[l])
        s5_bd, s5_cd = _s5_matrices(b_bar, s5_c_re[l], s5_c_im[l])
        lam_bar = lam_bar.transpose(1, 0, 2)
        lru_wg, lru_bg = _lru_gate_matrices(lru_w_a[l], lru_b_a[l], lru_w_x[l], lru_b_x[l])
        row = lambda a: a[l].reshape(1, -1)
        lw = {
            'g1': row(g_norm1), 'g2': row(g_norm2), 'g_final': g_final.reshape(1, D_MODEL),
            'w_mix': jnp.concatenate([w_in[l, :, :NA_WIDTH], w_in[l, :, 3 * NA_WIDTH:MIX_WIDTH]],
                                     axis=1).astype(BF16),
            'w_kv': w_in[l, :, NA_WIDTH:3 * NA_WIDTH].astype(BF16),
            'w_kv_t': w_in[l, :, NA_WIDTH:3 * NA_WIDTH].T.astype(BF16),
            'w_gate': w_in[l, :, MIX_WIDTH:].astype(BF16),
            's5_bd': s5_bd, 's5_cd': s5_cd, 's5_lam_bar': lam_bar, 's5_d': row(s5_d),
            's5_w_glu': s5_w_glu[l].astype(BF16),
            'lru_conv_w': lru_conv_w[l], 'lru_conv_b': row(lru_conv_b),
            'lru_wg': lru_wg, 'lru_bg': lru_bg, 'lru_lam': lru_lam[l],
            'w_br_attn': w_br_attn[l].astype(BF16), 'w_br_s5': w_br_s5[l].astype(BF16),
            'w_br_lru': w_br_lru[l].astype(BF16), 'w_out': w_out[l].astype(BF16),
            'ffn': (ffn_w_up[l].astype(BF16), ffn_conv_w[l], row(ffn_conv_b), ffn_w_down[l].astype(BF16)),
        }
        final = l == DEPTH - 1
        yp, (new_kt, new_vt, s5_last, lru_last) = _layer(
            yp, bsz, seq_len, mod[l], ctx_row, lw, functools.partial(_ctx_attention, layer=l),
            jnp.zeros((2, 2, bsz, S5_LANES), F32), jnp.zeros((2, bsz, LRU_WIDTH), F32),
            kv_caches, l, final)
        kv_caches = (new_kt, new_vt)
        s5s.append(s5_last)
        lrus.append(lru_last)
        s5_h0 = jnp.stack([state_s5_re[:, l], state_s5_im[:, l]], axis=0)
        s5_h0 = s5_h0.reshape(2, dec_bsz, 2, S5_LANES).transpose(2, 0, 1, 3)
        lru_h0 = state_lru[:, l].transpose(1, 0, 2)
        na = functools.partial(_na_attention, cache_k=cache_k, cache_v=cache_v, layer=l, rpb=rpb[l])
        ys, _ = _layer(ys, dec_bsz, dec_len, mod[l], dec_row, lw, na, s5_h0, lru_h0, None, l, final)

    s5_all = jnp.stack(s5s, axis=0)
    s5_all = s5_all.transpose(2, 3, 0, 1, 4).reshape(2, bsz, DEPTH, 2, S5_GROUPS, S5_STATE)
    new_lru = jnp.stack(lrus, axis=0).transpose(2, 0, 1, 3)
    cache_out = lambda a: a.reshape(bsz, DEPTH, NA_HEADS, HEAD_DIM, seq_len).transpose(0, 1, 4, 2, 3)
    return (yp.reshape(bsz, seq_len, D_MODEL), ys.reshape(dec_bsz, dec_len, D_MODEL),
            cache_out(kv_caches[0]), cache_out(kv_caches[1]), s5_all[0], s5_all[1], new_lru)
```

```python
import functools

import jax
import jax.numpy as jnp
from jax import lax
from jax.experimental import pallas as pl
from jax.experimental.pallas import tpu as pltpu

D_MODEL = 1024
DEPTH = 2
GRID_W = 64
NA_HEADS = 8
HEAD_DIM = 64
NA_WIDTH = NA_HEADS * HEAD_DIM
NA_MAX_ROWS = 8
NA_COLS = 16
S5_GROUPS = 16
S5_GROUP_CH = 16
S5_WIDTH = S5_GROUPS * S5_GROUP_CH
S5_STATE = 64
S5_LANES = S5_GROUPS * S5_STATE
LRU_WIDTH = 256
LRU_BLOCKS = 4
LRU_BLOCK = LRU_WIDTH // LRU_BLOCKS
LRU_C = 8.0
LRU_CONV = 4
FFN_DIM = 2816
FFN_CONV = 3
N_BRANCH = 3
MIX_WIDTH = 3 * NA_WIDTH + S5_WIDTH + 2 * LRU_WIDTH
EPS = 1e-6
NEG_INF = -1e30

SUBLANES = 8
COND_ROWS = 16
VMEM_LIMIT = 56 * 1024 * 1024

F32 = jnp.float32
BF16 = jnp.bfloat16


def _params(*semantics):
    return pltpu.CompilerParams(dimension_semantics=semantics, vmem_limit_bytes=VMEM_LIMIT)


def _dot(a, b):
    return jnp.dot(a, b, preferred_element_type=F32)


def _dot_nt(a, b):
    return lax.dot_general(a, b, (((1,), (1,)), ((), ())), preferred_element_type=F32)


def _rms(x):
    return x * lax.rsqrt(jnp.mean(x * x, axis=-1, keepdims=True) + EPS)


def _gelu(x):
    return jax.nn.gelu(x)


def _ada_kernel(cond_ref, w_ref, b_ref, o_ref):
    c = cond_ref[...]
    s = c * jax.nn.sigmoid(c)
    o_ref[0] = _dot(s.astype(BF16), w_ref[0].astype(BF16)) + b_ref[0]


def _ada(cond, w_ada, b_ada):
    nb = 1536
    n = 6 * D_MODEL
    return pl.pallas_call(
        _ada_kernel,
        grid=(DEPTH, n // nb),
        in_specs=[
            pl.BlockSpec((COND_ROWS, D_MODEL), lambda l, j: (0, 0)),
            pl.BlockSpec((1, D_MODEL, nb), lambda l, j: (l, 0, j)),
            pl.BlockSpec((1, 1, nb), lambda l, j: (l, 0, j)),
        ],
        out_specs=pl.BlockSpec((1, COND_ROWS, nb), lambda l, j: (l, 0, j)),
        out_shape=jax.ShapeDtypeStruct((DEPTH, COND_ROWS, n), F32),
        compiler_params=_params("arbitrary", "arbitrary"),
        name="ada",
    )(cond, w_ada, b_ada.reshape(DEPTH, 1, n))


def _in_kernel(x_ref, mod_ref, g_ref, w_ref, wkv_ref, cw_ref, cb_ref, *rest, seq_len, kv_transposed):
    q_ref, k_ref, v_ref, u_ref, xc_ref, xg_ref, pad_ref = rest[-7:]
    tm = x_ref.shape[0]
    md = mod_ref[0]
    hb = ((_rms(x_ref[...]) * g_ref[...]) * (1.0 + md[1:2]) + md[0:1]).astype(BF16)
    z = _dot(hb, w_ref[...])
    q_ref[...] = (z[:, 0:NA_WIDTH] * HEAD_DIM ** -0.5).astype(q_ref.dtype)
    if kv_transposed:
        kv = _dot_nt(wkv_ref[...], hb)
        for s in range(tm // seq_len):
            k_ref[s, 0] = kv[0:NA_WIDTH, s * seq_len:(s + 1) * seq_len]
            v_ref[s, 0] = kv[NA_WIDTH:2 * NA_WIDTH, s * seq_len:(s + 1) * seq_len]
    else:
        kv = _dot(hb, wkv_ref[...])
        k_ref[...] = kv[:, 0:NA_WIDTH].astype(k_ref.dtype)
        v_ref[...] = kv[:, NA_WIDTH:2 * NA_WIDTH].astype(v_ref.dtype)
    o = NA_WIDTH
    u_ref[...] = z[:, o:o + S5_WIDTH]
    xg_ref[...] = z[:, o + S5_WIDTH + LRU_WIDTH:o + S5_WIDTH + 2 * LRU_WIDTH]
    xr = z[:, o + S5_WIDTH:o + S5_WIDTH + LRU_WIDTH]
    pad_ref[0:SUBLANES, :] = jnp.zeros((SUBLANES, LRU_WIDTH), F32)
    pad_ref[SUBLANES + tm:2 * SUBLANES + tm, :] = jnp.zeros((SUBLANES, LRU_WIDTH), F32)
    pad_ref[SUBLANES:SUBLANES + tm, :] = xr
    t = lax.broadcasted_iota(jnp.int32, (tm, 1), 0) & (seq_len - 1)
    acc = jnp.zeros((tm, LRU_WIDTH), F32) + cb_ref[...]
    pad_l = LRU_CONV // 2
    for kk in range(LRU_CONV):
        s = kk - pad_l
        xs = xr if s == 0 else pad_ref[SUBLANES + s:SUBLANES + s + tm, :]
        valid = jnp.logical_and(t + s >= 0, t + s < seq_len)
        acc = acc + cw_ref[kk:kk + 1, :] * jnp.where(valid, xs, 0.0)
    xc_ref[...] = acc


def _in_proj(x, mod6, mod_row, g, w_mix, w_kv, conv_w, conv_b, seq_len, kv_cache, layer):
    t_tokens = x.shape[0]
    tm = 1024
    tok = lambda i: (i, 0)
    full = lambda i: (0, 0)
    tok_spec = lambda w: pl.BlockSpec((tm, w), tok)
    tok_shape = lambda w, d: jax.ShapeDtypeStruct((t_tokens, w), d)
    kv_transposed = kv_cache is not None
    if kv_transposed:
        nseq = tm // seq_len
        kv_spec = pl.BlockSpec((nseq, 1, NA_WIDTH, seq_len), lambda i: (i, layer, 0, 0))
        kv_shape = jax.ShapeDtypeStruct((t_tokens // seq_len, DEPTH, NA_WIDTH, seq_len), F32)
        kv_cache = tuple(kv_cache)
    else:
        kv_spec, kv_shape, kv_cache = tok_spec(NA_WIDTH), tok_shape(NA_WIDTH, BF16), ()
    n_in = 7
    return pl.pallas_call(
        functools.partial(_in_kernel, seq_len=seq_len, kv_transposed=kv_transposed),
        grid=(t_tokens // tm,),
        in_specs=[
            pl.BlockSpec((tm, D_MODEL), tok),
            pl.BlockSpec((1, 6, D_MODEL), lambda i: (mod_row(i, tm), 0, 0)),
            pl.BlockSpec((1, D_MODEL), full),
            pl.BlockSpec(w_mix.shape, full),
            pl.BlockSpec(w_kv.shape, full),
            pl.BlockSpec((LRU_CONV, LRU_WIDTH), full),
            pl.BlockSpec((1, LRU_WIDTH), full),
        ] + [pl.BlockSpec(memory_space=pl.ANY) for _ in kv_cache],
        out_specs=[tok_spec(NA_WIDTH), kv_spec, kv_spec,
                   tok_spec(S5_WIDTH), tok_spec(LRU_WIDTH), tok_spec(LRU_WIDTH)],
        out_shape=[tok_shape(NA_WIDTH, BF16), kv_shape, kv_shape,
                   tok_shape(S5_WIDTH, F32), tok_shape(LRU_WIDTH, F32), tok_shape(LRU_WIDTH, F32)],
        input_output_aliases={n_in + n: 1 + n for n in range(len(kv_cache))},
        scratch_shapes=[pltpu.VMEM((tm + 2 * SUBLANES, LRU_WIDTH), F32)],
        compiler_params=_params("arbitrary"),
        name="in_proj",
    )(x, mod6, g, w_mix, w_kv, conv_w, conv_b, *kv_cache)


def _softmax_rows(parts):
    m = parts[0].max(axis=-1, keepdims=True)
    for p in parts[1:]:
        m = jnp.maximum(m, p.max(axis=-1, keepdims=True))
    es = [jnp.exp(p - m) for p in parts]
    den = es[0].sum(axis=-1, keepdims=True)
    for e in es[1:]:
        den = den + e.sum(axis=-1, keepdims=True)
    return [e.astype(BF16) for e in es], 1.0 / den


def _head_pair_queries(q2):
    first = lax.broadcasted_iota(jnp.int32, (1, 2 * HEAD_DIM), 1) < HEAD_DIM
    zero = jnp.zeros_like(q2)
    return jnp.concatenate([jnp.where(first, q2, zero), jnp.where(first, zero, q2)], axis=0), first


def _ctx_attn_kernel(q_ref, kt_ref, vt_ref, o_ref, s_ref):
    length = q_ref.shape[1]
    pairs = NA_HEADS // 2
    cols = lambda j: slice(2 * HEAD_DIM * j, 2 * HEAD_DIM * (j + 1))
    first = None

    def scores(j):
        qs, is_first = _head_pair_queries(q_ref[0, :, cols(j)])
        s_ref[j % 2] = _dot(qs, kt_ref[cols(j), :].astype(BF16))
        return is_first

    def finish(j):
        (p,), inv = _softmax_rows([s_ref[j % 2]])
        o = _dot_nt(p, vt_ref[cols(j), :].astype(BF16)) * inv
        o_ref[0, :, cols(j)] = jnp.where(first, o[0:length], o[length:2 * length]).astype(o_ref.dtype)

    first = scores(0)
    for j in range(pairs):
        if j + 1 < pairs:
            scores(j + 1)
        finish(j)


def _ctx_attention(q, kt, vt, layer):
    bsz, length, _ = q.shape
    spec = pl.BlockSpec((1, length, NA_WIDTH), lambda b: (b, 0, 0))
    tspec = pl.BlockSpec((None, None, NA_WIDTH, length), lambda b: (b, layer, 0, 0))
    return pl.pallas_call(
        _ctx_attn_kernel,
        grid=(bsz,),
        in_specs=[spec, tspec, tspec],
        out_specs=spec,
        out_shape=jax.ShapeDtypeStruct((bsz, length, NA_WIDTH), BF16),
        scratch_shapes=[pltpu.VMEM((2, 2 * length, length), F32)],
        compiler_params=_params("arbitrary"),
        name="ctx_attn",
    )(q, kt, vt)


def _na_rows(length):
    rows = length // GRID_W
    return rows, min(NA_MAX_ROWS, rows)


N_DY = 2 * NA_MAX_ROWS - 1
N_DX = 2 * NA_COLS - 1


def _na_build_bias(rpb_ref, bias_ref, pair):
    shape = (2 * GRID_W, 2 * GRID_W)
    row = lax.broadcasted_iota(jnp.int32, shape, 0)
    lane = lax.broadcasted_iota(jnp.int32, shape, 1)
    qc = row & (GRID_W - 1)
    kc = lane & (GRID_W - 1)
    top = row < GRID_W
    dx = jnp.clip(kc - qc + NA_COLS - 1, 0, N_DX - 1)
    win0 = jnp.clip(qc - NA_COLS // 2, 0, GRID_W - NA_COLS)
    in_win = jnp.logical_and(kc >= win0, kc < win0 + NA_COLS)

    def one(dy, carry):
        base0 = ((2 * pair) * N_DY + dy) * N_DX
        base1 = base0 + N_DY * N_DX
        val = jnp.full(shape, NEG_INF, F32)
        for j in range(N_DX):
            val = jnp.where(dx == j, jnp.where(top, rpb_ref[base0 + j], rpb_ref[base1 + j]), val)
        bias_ref[dy] = jnp.where(in_win, val, NEG_INF)
        return carry

    lax.fori_loop(0, N_DY, one, 0)

    left = lane < GRID_W

    def pair_up(dy, carry):
        bias_ref[dy] = jnp.where(left, bias_ref[dy], bias_ref[dy + 1])
        return carry

    lax.fori_loop(0, N_DY - 1, pair_up, 0)


def _na_attn_kernel(rpb_ref, q_ref, k_ref, v_ref, kct_ref, vct_ref, o_ref,
                    bias_ref, s_ref, kcb_ref, vcb_ref):
    rows, wr = _na_rows(q_ref.shape[1])
    nloc = wr * GRID_W

    @pl.when(pl.program_id(1) == 0)
    def _():
        _na_build_bias(rpb_ref, bias_ref, pl.program_id(0))

    kcb_ref[...] = kct_ref[...].astype(BF16)
    vcb_ref[...] = vct_ref[...].astype(BF16)
    first = lax.broadcasted_iota(jnp.int32, (1, 2 * HEAD_DIM), 1) < HEAD_DIM
    start = lambda r: min(max(r - wr // 2, 0), rows - wr)

    def scores(r):
        qs, _ = _head_pair_queries(q_ref[0, r * GRID_W:(r + 1) * GRID_W, :])
        kl = k_ref[0, start(r) * GRID_W:start(r) * GRID_W + nloc, :]
        dy0 = start(r) - r + NA_MAX_ROWS - 1
        bias = jnp.concatenate([bias_ref[dy0 + 2 * m] for m in range(wr // 2)], axis=-1)
        s_ref[r % 2, :, 0:nloc] = _dot_nt(qs, kl) + bias
        s_ref[r % 2, :, nloc:] = _dot(qs, kcb_ref[...])

    def finish(r):
        (p_loc, p_ctx), inv = _softmax_rows([s_ref[r % 2, :, 0:nloc], s_ref[r % 2, :, nloc:]])
        vl = v_ref[0, start(r) * GRID_W:start(r) * GRID_W + nloc, :]
        o = (_dot(p_loc, vl) + _dot_nt(p_ctx, vcb_ref[...])) * inv
        o_ref[0, r * GRID_W:(r + 1) * GRID_W, :] = jnp.where(
            first, o[0:GRID_W], o[GRID_W:2 * GRID_W]).astype(o_ref.dtype)

    scores(0)
    for r in range(rows):
        if r + 1 < rows:
            scores(r + 1)
        finish(r)


def _na_attention(q, k, v, cache_k, cache_v, layer, rpb):
    bsz, length, _ = q.shape
    ctx = cache_k.shape[3]
    rows, wr = _na_rows(length)
    assert wr % 2 == 0 and GRID_W == HEAD_DIM
    pair = 2 * HEAD_DIM
    tok = pl.BlockSpec((1, length, pair), lambda j, b: (b, 0, j))
    cache = pl.BlockSpec((None, None, pair, ctx), lambda j, b: (b, layer, j, 0))
    return pl.pallas_call(
        _na_attn_kernel,
        grid=(NA_HEADS // 2, bsz),
        in_specs=[pl.BlockSpec(memory_space=pltpu.SMEM), tok, tok, tok, cache, cache],
        out_specs=tok,
        out_shape=jax.ShapeDtypeStruct((bsz, length, NA_WIDTH), BF16),
        scratch_shapes=[pltpu.VMEM((N_DY, 2 * GRID_W, 2 * GRID_W), F32),
                        pltpu.VMEM((2, 2 * GRID_W, wr * GRID_W + ctx), F32),
                        pltpu.VMEM((pair, ctx), BF16), pltpu.VMEM((pair, ctx), BF16)],
        compiler_params=_params("arbitrary", "arbitrary"),
        name="na_attn",
    )(rpb.reshape(-1), q, k, v, cache_k, cache_v)


LANES = 128


def _to_time_major(src_ref, dst_ref):
    tc = src_ref.shape[1]
    for b in range(SUBLANES):
        for j in range(dst_ref.shape[0]):
            dst_ref[j, pl.ds(b, tc, stride=SUBLANES), :] = src_ref[b, :, j * LANES:(j + 1) * LANES]
    return jnp.concatenate([dst_ref[j] for j in range(dst_ref.shape[0])], axis=-1)


def _from_time_major(src_ref, store):
    tc = src_ref.shape[1] // SUBLANES
    for b in range(SUBLANES):
        for j in range(src_ref.shape[0]):
            store(b, j, src_ref[j, pl.ds(b, tc, stride=SUBLANES), :])


def _s5_prep_kernel(lre_ref, lim_ref, step_ref, bre_ref, bim_ref, lam_ref, bb_ref):
    lre = lre_ref[...]
    lim = lim_ref[...]
    step = jnp.exp(step_ref[...])
    mag = jnp.exp(lre * step)
    ang = lim * step
    ar = mag * jnp.cos(ang)
    ai = mag * jnp.sin(ang)
    lam_ref[0] = ar
    lam_ref[1] = ai
    den = lre * lre + lim * lim
    kr = ((ar - 1.0) * lre + ai * lim) / den
    ki = (ai * lre - (ar - 1.0) * lim) / den
    for d in range(2):
        bb_ref[d, 0] = kr[d:d + 1] * bre_ref[...] - ki[d:d + 1] * bim_ref[...]
        bb_ref[d, 1] = kr[d:d + 1] * bim_ref[...] + ki[d:d + 1] * bre_ref[...]


def _s5_prep(lam_re, lam_im, log_step, b_re, b_im):
    step = jnp.broadcast_to(log_step[..., None], lam_re.shape).reshape(2, S5_LANES)
    bt = lambda b: b.transpose(2, 0, 1).reshape(S5_GROUP_CH, S5_LANES)
    return pl.pallas_call(
        _s5_prep_kernel,
        out_shape=[jax.ShapeDtypeStruct((2, 2, S5_LANES), F32),
                   jax.ShapeDtypeStruct((2, 2, S5_GROUP_CH, S5_LANES), F32)],
        name="s5_prep",
    )(lam_re.reshape(2, S5_LANES), lam_im.reshape(2, S5_LANES), step, bt(b_re), bt(b_im))


def _s5_matrices(b_bar, c_re, c_im):
    eye = jnp.eye(S5_GROUPS, dtype=F32)
    bb = b_bar.reshape(2, 2, S5_GROUP_CH, S5_GROUPS, S5_STATE)
    bd = jnp.einsum('dcpgn,gh->dgpchn', bb, eye)
    bd = bd.reshape(2, S5_WIDTH, 2 * S5_LANES).astype(BF16)
    cc = jnp.stack([c_re, c_im], axis=1)
    cd = jnp.einsum('dcgpn,gh->dcgnhp', cc, eye)
    cd = cd.reshape(2, 2, S5_LANES, S5_WIDTH).astype(BF16)
    return bd, cd


def _s5_kernel(u_ref, bd_ref, cd_ref, lam_ref, dvec_ref, h0_ref, y_ref, hl_ref,
               ui_ref, bu_ref, yi_ref, st_ref, *, tc):
    d = pl.program_id(1)
    c = pl.program_id(2)

    @pl.when(c == 0)
    def _():
        st_ref[...] = h0_ref[0]

    _to_time_major(u_ref, ui_ref)
    shape = (SUBLANES, S5_LANES)
    lr = jnp.broadcast_to(lam_ref[0, 0:1, :], shape)
    li = jnp.broadcast_to(lam_ref[0, 1:2, :], shape)
    half = tc // 2
    re = slice(0, S5_LANES)
    im = slice(S5_LANES, 2 * S5_LANES)

    def run(reverse):
        halves = (1, 0) if reverse else (0, 1)
        rows = lambda hf: slice(hf * half * SUBLANES, (hf + 1) * half * SUBLANES)

        def u_rows(hf):
            return jnp.concatenate([ui_ref[j, rows(hf), :] for j in range(ui_ref.shape[0])], axis=-1)

        def project(hf):
            bu_ref[rows(hf), :] = _dot(u_rows(hf).astype(BF16), bd_ref[0])

        def scan(hf, hr, hi):
            steps = range(half - 1, -1, -1) if reverse else range(half)
            for t in steps:
                r = slice((hf * half + t) * SUBLANES, (hf * half + t + 1) * SUBLANES)
                nr = (lr * hr - li * hi) + bu_ref[r, re]
                ni = (lr * hi + li * hr) + bu_ref[r, im]
                bu_ref[r, re] = nr
                bu_ref[r, im] = ni
                hr, hi = nr, ni
            return hr, hi

        def readout(hf):
            hb = bu_ref[rows(hf), :].astype(BF16)
            y = _dot(hb[:, re], cd_ref[0, 0]) - _dot(hb[:, im], cd_ref[0, 1])
            if not reverse:
                y = y + dvec_ref[...] * u_rows(hf)
            for j in range(yi_ref.shape[0]):
                yi_ref[j, rows(hf), :] = y[:, j * LANES:(j + 1) * LANES]

        project(halves[0])
        project(halves[1])
        hr, hi = scan(halves[0], st_ref[0], st_ref[1])
        readout(halves[0])
        hr, hi = scan(halves[1], hr, hi)
        readout(halves[1])
        st_ref[0] = hr
        st_ref[1] = hi
        hl_ref[0, 0] = hr
        hl_ref[0, 1] = hi

    @pl.when(d == 0)
    def _():
        run(False)

    @pl.when(d == 1)
    def _():
        run(True)

    def store(b, j, val):
        y_ref[0, b, :, j * LANES:(j + 1) * LANES] = val

    _from_time_major(yi_ref, store)


def _s5_scan(u, bd, cd, lam_bar, dvec, h0):
    bsz, length, _ = u.shape
    tc = 128
    nc = length // tc
    chunk = lambda d, c: c + d * (nc - 1 - 2 * c)
    return pl.pallas_call(
        functools.partial(_s5_kernel, tc=tc),
        grid=(bsz // SUBLANES, 2, nc),
        in_specs=[
            pl.BlockSpec((SUBLANES, tc, S5_WIDTH), lambda g, d, c: (g, chunk(d, c), 0)),
            pl.BlockSpec((1, S5_WIDTH, 2 * S5_LANES), lambda g, d, c: (d, 0, 0)),
            pl.BlockSpec((1, 2, S5_LANES, S5_WIDTH), lambda g, d, c: (d, 0, 0, 0)),
            pl.BlockSpec((1, 2, S5_LANES), lambda g, d, c: (d, 0, 0)),
            pl.BlockSpec((1, S5_WIDTH), lambda g, d, c: (0, 0)),
            pl.BlockSpec((1, 2, SUBLANES, S5_LANES), lambda g, d, c: (d, 0, g, 0)),
        ],
        out_specs=[
            pl.BlockSpec((1, SUBLANES, tc, S5_WIDTH), lambda g, d, c: (d, g, chunk(d, c), 0)),
            pl.BlockSpec((1, 2, SUBLANES, S5_LANES), lambda g, d, c: (d, 0, g, 0)),
        ],
        out_shape=[jax.ShapeDtypeStruct((2, bsz, length, S5_WIDTH), F32),
                   jax.ShapeDtypeStruct((2, 2, bsz, S5_LANES), F32)],
        scratch_shapes=[
            pltpu.VMEM((S5_WIDTH // LANES, tc * SUBLANES, LANES), F32),
            pltpu.VMEM((tc * SUBLANES, 2 * S5_LANES), F32),
            pltpu.VMEM((S5_WIDTH // LANES, tc * SUBLANES, LANES), F32),
            pltpu.VMEM((2, SUBLANES, S5_LANES), F32),
        ],
        compiler_params=_params("arbitrary", "arbitrary", "arbitrary"),
        name="s5_scan",
    )(u, bd, cd, lam_bar, dvec, h0)


def _lru_kernel(xc_ref, wg_ref, bg_ref, lam_ref, h0_ref, h_ref, hl_ref,
                xi_ref, a_ref, b_ref, hi_ref, st_ref, *, tc):
    d = pl.program_id(1)
    c = pl.program_id(2)

    @pl.when(c == 0)
    def _():
        st_ref[...] = h0_ref[0]

    x = _to_time_major(xc_ref, xi_ref)
    gates =_dot(x.astype(BF16), wg_ref[0]) + bg_ref[0]
    sigmoid = lambda g: 0.5 * jnp.tanh(0.5 * g) + 0.5
    r = sigmoid(gates[:, 0:LRU_WIDTH])
    i = sigmoid(gates[:, LRU_WIDTH:2 * LRU_WIDTH])
    z = -lam_ref[0]
    softplus = jnp.maximum(z, 0.0) + jnp.log1p(jnp.exp(-jnp.abs(z)))
    log_a = -LRU_C * r * softplus
    th = jnp.tanh(log_a)
    neg_expm1 = -2.0 * th / (1.0 - th)
    a_ref[...] = jnp.exp(log_a)
    b_ref[...] = jnp.sqrt(neg_expm1) * (i * x)

    def step(t, h):
        tt = jnp.where(d == 0, t, tc - 1 - t)
        rows = pl.ds(pl.multiple_of(tt * SUBLANES, SUBLANES), SUBLANES)
        h = a_ref[rows, :] * h + b_ref[rows, :]
        for j in range(hi_ref.shape[0]):
            hi_ref[j, rows, :] = h[:, j * LANES:(j + 1) * LANES]
        return h

    h = lax.fori_loop(0, tc, step, st_ref[...], unroll=8)
    st_ref[...] = h
    hl_ref[0] = h

    def store(b, j, val):
        h_ref[0, b, :, j * LANES:(j + 1) * LANES] = val

    _from_time_major(hi_ref, store)


def _lru_gate_matrices(w_a, b_a, w_x, b_x):
    eye = jnp.eye(LRU_BLOCKS, dtype=F32)
    dense = lambda w: jnp.einsum('dhij,hk->dhikj', w, eye).reshape(2, LRU_WIDTH, LRU_WIDTH)
    wg = jnp.concatenate([dense(w_a), dense(w_x)], axis=-1).astype(BF16)
    bg = jnp.concatenate([b_a, b_x], axis=-1).reshape(2, 1, 2 * LRU_WIDTH)
    return wg, bg


def _lru_scan(xc, wg, bg, lam, h0):
    bsz, length, _ = xc.shape
    tc = 128
    nc = length // tc
    chunk = lambda d, c: c + d * (nc - 1 - 2 * c)
    rows = tc * SUBLANES
    return pl.pallas_call(
        functools.partial(_lru_kernel, tc=tc),
        grid=(bsz // SUBLANES, 2, nc),
        in_specs=[
            pl.BlockSpec((SUBLANES, tc, LRU_WIDTH), lambda g, d, c: (g, chunk(d, c), 0)),
            pl.BlockSpec((1, LRU_WIDTH, 2 * LRU_WIDTH), lambda g, d, c: (d, 0, 0)),
            pl.BlockSpec((1, 1, 2 * LRU_WIDTH), lambda g, d, c: (d, 0, 0)),
            pl.BlockSpec((1, 1, LRU_WIDTH), lambda g, d, c: (d, 0, 0)),
            pl.BlockSpec((1, SUBLANES, LRU_WIDTH), lambda g, d, c: (d, g, 0)),
        ],
        out_specs=[
            pl.BlockSpec((1, SUBLANES, tc, LRU_WIDTH), lambda g, d, c: (d, g, chunk(d, c), 0)),
            pl.BlockSpec((1, SUBLANES, LRU_WIDTH), lambda g, d, c: (d, g, 0)),
        ],
        out_shape=[jax.ShapeDtypeStruct((2, bsz, length, LRU_WIDTH), F32),
                   jax.ShapeDtypeStruct((2, bsz, LRU_WIDTH), F32)],
        scratch_shapes=[
            pltpu.VMEM((LRU_WIDTH // LANES, rows, LANES), F32),
            pltpu.VMEM((rows, LRU_WIDTH), F32),
            pltpu.VMEM((rows, LRU_WIDTH), F32),
            pltpu.VMEM((LRU_WIDTH // LANES, rows, LANES), F32),
            pltpu.VMEM((SUBLANES, LRU_WIDTH), F32),
        ],
        compiler_params=_params("arbitrary", "arbitrary", "arbitrary"),
        name="lru_scan",
    )(xc, wg, bg, lam.reshape(2, 1, LRU_WIDTH), h0)


def _merge_kernel(x_ref, mod_ref, g1_ref, g2_ref, attn_ref, s5_ref, lru_ref, xg_ref,
                  wgate_ref, wglu_ref, wba_ref, wbs_ref, wbl_ref, wout_ref, x1_ref, h2_ref):
    md = mod_ref[0]
    x = x_ref[...]
    h = (_rms(x) * g1_ref[...]) * (1.0 + md[1:2]) + md[0:1]
    gates = jax.nn.sigmoid(_dot(h.astype(BF16), wgate_ref[...]))
    ys = _gelu(s5_ref[0] + s5_ref[1])
    s5_y = ys * jax.nn.sigmoid(_dot(ys.astype(BF16), wglu_ref[...]))
    lru_y = (lru_ref[0] + lru_ref[1]) * _gelu(xg_ref[...])
    merged = (gates[:, 0:D_MODEL] * _dot(attn_ref[...], wba_ref[...])
              + gates[:, D_MODEL:2 * D_MODEL] * _dot(s5_y.astype(BF16), wbs_ref[...])
              + gates[:, 2 * D_MODEL:3 * D_MODEL] * _dot(lru_y.astype(BF16), wbl_ref[...]))
    x1 = x + md[2:3] * _dot(merged.astype(BF16), wout_ref[...])
    x1_ref[...] = x1
    h2_ref[...] = ((_rms(x1) * g2_ref[...]) * (1.0 + md[4:5]) + md[3:4]).astype(h2_ref.dtype)


def _merge(x, mod6, mod_row, g1, g2, attn, s5_y, lru_h, xg, w_gate, w_glu, w_ba, w_bs, w_bl, w_out):
    t_tokens = x.shape[0]
    tm = 512
    tok = lambda i: (i, 0)
    full = lambda i: (0, 0)
    wspec = lambda w: pl.BlockSpec(w.shape, full)
    return pl.pallas_call(
        _merge_kernel,
        grid=(t_tokens // tm,),
        in_specs=[
            pl.BlockSpec((tm, D_MODEL), tok),
            pl.BlockSpec((1, 6, D_MODEL), lambda i: (mod_row(i, tm), 0, 0)),
            pl.BlockSpec((1, D_MODEL), full),
            pl.BlockSpec((1, D_MODEL), full),
            pl.BlockSpec((tm, NA_WIDTH), tok),
            pl.BlockSpec((2, tm, S5_WIDTH), lambda i: (0, i, 0)),
            pl.BlockSpec((2, tm, LRU_WIDTH), lambda i: (0, i, 0)),
            pl.BlockSpec((tm, LRU_WIDTH), tok),
            wspec(w_gate), wspec(w_glu), wspec(w_ba), wspec(w_bs), wspec(w_bl), wspec(w_out),
        ],
        out_specs=[pl.BlockSpec((tm, D_MODEL), tok), pl.BlockSpec((tm, D_MODEL), tok)],
        out_shape=[jax.ShapeDtypeStruct((t_tokens, D_MODEL), F32),
                   jax.ShapeDtypeStruct((t_tokens, D_MODEL), BF16)],
        compiler_params=_params("arbitrary"),
        name="merge",
    )(x, mod6, g1, g2, attn, s5_y, lru_h, xg, w_gate, w_glu, w_ba, w_bs, w_bl, w_out)


FFN_CHUNK = 256
FFN_ROWS = 256
FFN_LAG = 1


def _ffn_kernel(h_ref, x_ref, mod_ref, wup_ref, cw_ref, cb_ref, wd_ref, gf_ref, o_ref,
                acc_ref, pad_ref, b_ref, *, seq_len, final):
    tm = h_ref.shape[0]
    fc = FFN_CHUNK
    nj = FFN_DIM // fc
    lo = SUBLANES
    t = lax.broadcasted_iota(jnp.int32, (tm, 1), 0) & (seq_len - 1)
    cols = lambda j: slice(j * fc, (j + 1) * fc)
    rb = FFN_ROWS
    rows = lambda i: slice(i * rb, (i + 1) * rb)

    def up(j, i):
        h = h_ref[rows(i), :]
        pad_ref[j % 2, lo + i * rb:lo + (i + 1) * rb, :] = _dot(h, wup_ref[:, cols(j)])
        b_ref[j % 2, rows(i), :] = _dot(h, wup_ref[:, cols(j + nj)])

    def finish(j, i):
        slot = j % 2
        r0 = lo + i * rb
        a = pad_ref[slot, r0:r0 + rb, :]
        prev = pad_ref[slot, r0 - 1:r0 - 1 + rb, :]
        nxt = pad_ref[slot, r0 + 1:r0 + 1 + rb, :]
        prev = jnp.where(t[rows(i)] >= 1, prev, 0.0)
        nxt = jnp.where(t[rows(i)] < seq_len - 1, nxt, 0.0)
        cw = cw_ref[:, cols(j)]
        conv = cw[0:1, :] * prev + cw[1:2, :] * a + cw[2:3, :] * nxt + cb_ref[:, cols(j)]
        act = (_gelu(conv) * b_ref[slot, rows(i), :]).astype(BF16)
        part = _dot(act, wd_ref[cols(j), :])
        if j == 0:
            acc_ref[rows(i), :] = part
        else:
            acc_ref[rows(i), :] += part

    for slot in range(2):
        pad_ref[slot, 0:lo, :] = jnp.zeros((lo, fc), F32)
        pad_ref[slot, lo + tm:2 * lo + tm, :] = jnp.zeros((lo, fc), F32)
    for i in range(tm // rb):
        up(0, i)
    nblk = tm // rb
    for j in range(nj):
        for i in range(nblk + FFN_LAG):
            if j + 1 < nj and i < nblk:
                up(j + 1, i)
            if i >= FFN_LAG:
                finish(j, i - FFN_LAG)
    x2 = x_ref[...] + mod_ref[0][5:6] * acc_ref[...]
    if final:
        x2 = _rms(x2) * gf_ref[...]
    o_ref[...] = x2


def _ffn(h2, x1, mod6, mod_row, wup, cw, cb, wd, g_final, seq_len, final):
    t_tokens = x1.shape[0]
    tm = 1024
    tok = lambda i: (i, 0)
    whole = lambda w: pl.BlockSpec(w.shape, lambda i: (0,) * w.ndim)
    return pl.pallas_call(
        functools.partial(_ffn_kernel, seq_len=seq_len, final=final),
        grid=(t_tokens // tm,),
        in_specs=[
            pl.BlockSpec((tm, D_MODEL), tok),
            pl.BlockSpec((tm, D_MODEL), tok),
            pl.BlockSpec((1, 6, D_MODEL), lambda i: (mod_row(i, tm), 0, 0)),
            whole(wup), whole(cw), whole(cb), whole(wd), whole(g_final),
        ],
        out_specs=pl.BlockSpec((tm, D_MODEL), tok),
        out_shape=jax.ShapeDtypeStruct((t_tokens, D_MODEL), F32),
        scratch_shapes=[pltpu.VMEM((tm, D_MODEL), F32),
                        pltpu.VMEM((2, tm + 2 * SUBLANES, FFN_CHUNK), F32),
                        pltpu.VMEM((2, tm, FFN_CHUNK), F32)],
        compiler_params=_params("arbitrary"),
        name="ffn",
    )(h2, x1, mod6, wup, cw, cb, wd, g_final)


def _layer(x, bsz, length, mod6, mod_row, lw, attention, s5_h0, lru_h0, kv_cache, layer, final):
    q, k, v, u, xc, xg = _in_proj(x, mod6, mod_row, lw['g1'], lw['w_mix'],
                                  lw['w_kv'] if kv_cache is None else lw['w_kv_t'],
                                  lw['lru_conv_w'], lw['lru_conv_b'], length, kv_cache, layer)
    seq = lambda a: a.reshape(bsz, length, a.shape[-1])
    kv = seq if kv_cache is None else (lambda a: a)
    attn = attention(seq(q), kv(k), kv(v)).reshape(bsz * length, NA_WIDTH)
    s5_y, s5_last = _s5_scan(seq(u), lw['s5_bd'], lw['s5_cd'], lw['s5_lam_bar'], lw['s5_d'], s5_h0)
    lru_h, lru_last = _lru_scan(seq(xc), lw['lru_wg'], lw['lru_bg'], lw['lru_lam'], lru_h0)
    x1, h2 = _merge(x, mod6, mod_row, lw['g1'], lw['g2'], attn,
                    s5_y.reshape(2, bsz * length, S5_WIDTH), lru_h.reshape(2, bsz * length, LRU_WIDTH),
                    xg, lw['w_gate'], lw['s5_w_glu'], lw['w_br_attn'], lw['w_br_s5'], lw['w_br_lru'],
                    lw['w_out'])
    x2 = _ffn(h2, x1, mod6, mod_row, *lw['ffn'], lw['g_final'], length, final)
    return x2, (k, v, s5_last, lru_last)


def kernel(x_prompt, x_sample, cache_k, cache_v, state_s5_re, state_s5_im, state_lru, c, c_ctx,
           w_ada, b_ada, g_norm1, g_norm2, w_in, rpb,
           s5_lam_re, s5_lam_im, s5_log_step, s5_b_re, s5_b_im, s5_c_re, s5_c_im, s5_d, s5_w_glu,
           lru_conv_w, lru_conv_b, lru_w_a, lru_b_a, lru_w_x, lru_b_x, lru_lam,
           w_br_attn, w_br_s5, w_br_lru, w_out,
           ffn_w_up, ffn_conv_w, ffn_conv_b, ffn_w_down, g_final):
    bsz, seq_len, _ = x_prompt.shape
    dec_bsz, dec_len, _ = x_sample.shape
    past = cache_k.shape[2]
    assert dec_bsz + 1 <= COND_ROWS and bsz % SUBLANES == 0 and dec_bsz % SUBLANES == 0

    cond = jnp.zeros((COND_ROWS, D_MODEL), F32).at[:dec_bsz].set(c).at[dec_bsz].set(c_ctx)
    mod = _ada(cond, w_ada, b_ada).reshape(DEPTH, COND_ROWS, 6, D_MODEL)
    ctx_row = lambda i, tm: dec_bsz
    dec_row = lambda i, tm: (i * tm) // dec_len

    cache_t = lambda a: a.transpose(0, 1, 3, 4, 2).reshape(dec_bsz, DEPTH, NA_WIDTH, past)
    cache_k, cache_v = cache_t(cache_k), cache_t(cache_v)
    yp = x_prompt.reshape(bsz * seq_len, D_MODEL)
    ys = x_sample.reshape(dec_bsz * dec_len, D_MODEL)
    s5s, lrus = [], []
    kv_caches = ()
    for l in range(DEPTH):
        lam_bar, b_bar = _s5_prep(s5_lam_re[l], s5_lam_im[l], s5_log_step[l], s5_b_re[l], s5_b_im[l])
        s5_bd, s5_cd = _s5_matrices(b_bar, s5_c_re[l], s5_c_im[l])
        lam_bar = lam_bar.transpose(1, 0, 2)
        lru_wg, lru_bg = _lru_gate_matrices(lru_w_a[l], lru_b_a[l], lru_w_x[l], lru_b_x[l])
        row = lambda a: a[l].reshape(1, -1)
        lw = {
            'g1': row(g_norm1), 'g2': row(g_norm2), 'g_final': g_final.reshape(1, D_MODEL),
            'w_mix': jnp.concatenate([w_in[l, :, :NA_WIDTH], w_in[l, :, 3 * NA_WIDTH:MIX_WIDTH]],
                                     axis=1).astype(BF16),
            'w_kv': w_in[l, :, NA_WIDTH:3 * NA_WIDTH].astype(BF16),
            'w_kv_t': w_in[l, :, NA_WIDTH:3 * NA_WIDTH].T.astype(BF16),
            'w_gate': w_in[l, :, MIX_WIDTH:].astype(BF16),
            's5_bd': s5_bd, 's5_cd': s5_cd, 's5_lam_bar': lam_bar, 's5_d': row(s5_d),
            's5_w_glu': s5_w_glu[l].astype(BF16),
            'lru_conv_w': lru_conv_w[l], 'lru_conv_b': row(lru_conv_b),
            'lru_wg': lru_wg, 'lru_bg': lru_bg, 'lru_lam': lru_lam[l],
            'w_br_attn': w_br_attn[l].astype(BF16), 'w_br_s5': w_br_s5[l].astype(BF16),
            'w_br_lru': w_br_lru[l].astype(BF16), 'w_out': w_out[l].astype(BF16),
            'ffn': (ffn_w_up[l].astype(BF16), ffn_conv_w[l], row(ffn_conv_b), ffn_w_down[l].astype(BF16)),
        }
        final = l == DEPTH - 1
        yp, (new_kt, new_vt, s5_last, lru_last) = _layer(
            yp, bsz, seq_len, mod[l], ctx_row, lw, functools.partial(_ctx_attention, layer=l),
            jnp.zeros((2, 2, bsz, S5_LANES), F32), jnp.zeros((2, bsz, LRU_WIDTH), F32),
            kv_caches, l, final)
        kv_caches = (new_kt, new_vt)
        s5s.append(s5_last)
        lrus.append(lru_last)
        s5_h0 = jnp.stack([state_s5_re[:, l], state_s5_im[:, l]], axis=0)
        s5_h0 = s5_h0.reshape(2, dec_bsz, 2, S5_LANES).transpose(2, 0, 1, 3)
        lru_h0 = state_lru[:, l].transpose(1, 0, 2)
        na = functools.partial(_na_attention, cache_k=cache_k, cache_v=cache_v, layer=l, rpb=rpb[l])
        ys, _ = _layer(ys, dec_bsz, dec_len, mod[l], dec_row, lw, na, s5_h0, lru_h0, None, l, final)

    s5_all = jnp.stack(s5s, axis=0)
    s5_all = s5_all.transpose(2, 3, 0, 1, 4).reshape(2, bsz, DEPTH, 2, S5_GROUPS, S5_STATE)
    new_lru = jnp.stack(lrus, axis=0).transpose(2, 0, 1, 3)
    cache_out = lambda a: a.reshape(bsz, DEPTH, NA_HEADS, HEAD_DIM, seq_len).transpose(0, 1, 4, 2, 3)
    return (yp.reshape(bsz, seq_len, D_MODEL), ys.reshape(dec_bsz, dec_len, D_MODEL),
            cache_out(kv_caches[0]), cache_out(kv_caches[1]), s5_all[0], s5_all[1], new_lru)
```
